```python
import math
import jax
import jax.numpy as jnp
from jax import lax
import numpy as np

D_MODEL = 2048
BATCH = 1
SEQ = 16384
DEPTH = 2

GRID_W = 64
CTX_LEN = 256
N_MOD = 6
NORM_EPS = 1e-6

CONV_DIM = D_MODEL // 2
MLA_HEADS = 8
MLA_NOPE = 128
MLA_ROPE = 64
MLA_V = 128
MLA_QK = MLA_NOPE + MLA_ROPE
MLA_Q_RANK = D_MODEL // 4
MLA_KV_RANK = D_MODEL // 8
ATTN_SCALE = MLA_QK ** -0.5
ROPE_AXIS = MLA_ROPE // 2
ROPE_THETA = 10000.0
Q_BLOCK = 128
GM_DIM = D_MODEL // 2
GM_GROUPS = 8
GM_CHUNK = 128
HY_DIM = D_MODEL // 2
HY_BANDS = 16
HY_EMB = 1 + 2 * HY_BANDS
HY_FFN = 64
HY_TARGET = 1e-2
HY_FAST = 0.3
HY_SLOW = 1.5
PEER_KEYS = 128
PEER_EXPERTS = PEER_KEYS * PEER_KEYS
PEER_HEADS = 8
PEER_TOPK = 16
PEER_QDIM = 256
PEER_HALF = PEER_QDIM // 2
PEER_BLOCK = 128

IN0_COLS = 3 * CONV_DIM + MLA_Q_RANK + MLA_KV_RANK + MLA_ROPE
IN1_COLS = 2 * GM_DIM + 3 * HY_DIM
MIX0_WIDTH = CONV_DIM + MLA_HEADS * MLA_V
MIX1_WIDTH = GM_DIM + HY_DIM

kernel_name = 'hybrid_diffusion_conv_mla_gmlp_hyena_peer'


def rmsnorm(x, g):
    xf = x.astype(jnp.float32)
    y = xf * lax.rsqrt(jnp.mean(xf * xf, axis=-1, keepdims=True) + NORM_EPS)
    return (y * g.astype(jnp.float32)).astype(x.dtype)


def modulate(h, shift, scale):
    return h * (1 + scale) + shift


def dwconv3(u, w, b=None):
    up = jnp.pad(u, ((0, 0), (1, 1), (0, 0)))
    y = up[:, :-2] * w[0] + up[:, 1:-1] * w[1] + up[:, 2:] * w[2]
    return y if b is None else y + b


def axial_rope_tables(n_rows):
    rr, cc = jnp.meshgrid(jnp.arange(n_rows), jnp.arange(GRID_W), indexing='ij')
    inv = ROPE_THETA ** (-jnp.arange(ROPE_AXIS // 2, dtype=jnp.float32) / (ROPE_AXIS // 2))
    ar = rr.reshape(-1, 1).astype(jnp.float32) * inv
    ac = cc.reshape(-1, 1).astype(jnp.float32) * inv
    ang = jnp.concatenate([ar, ar, ac, ac], axis=-1)
    return jnp.cos(ang), jnp.sin(ang)


def rope_2d(u, cos, sin):
    us = u.reshape(u.shape[:-1] + (2, 2, ROPE_AXIS // 2))
    rot = jnp.stack([-us[..., 1, :], us[..., 0, :]], axis=-2).reshape(u.shape)
    out = u * cos[None, :, None, :] + rot * sin[None, :, None, :]
    return out.astype(u.dtype)


def mla_keys_values(kv_c, k_r, kv_norm_g, w_ukv, rope=None):
    B, L, _ = kv_c.shape
    kv = (rmsnorm(kv_c, kv_norm_g) @ w_ukv).reshape(B, L, MLA_HEADS, MLA_NOPE + MLA_V)
    k_r = k_r[:, :, None, :]
    if rope is not None:
        k_r = rope_2d(k_r, *rope)
    k = jnp.concatenate([kv[..., :MLA_NOPE], jnp.broadcast_to(k_r, (B, L, MLA_HEADS, MLA_ROPE))], axis=-1)
    return k, kv[..., MLA_NOPE:]


def block_attention(q, k, v):
    B, L, H, _ = q.shape
    qb = q.reshape(B, L // Q_BLOCK, Q_BLOCK, H, MLA_QK).transpose(1, 0, 2, 3, 4)

    def one_block(qi):
        s = jnp.einsum('bqhd,bkhd->bhqk', qi, k, preferred_element_type=jnp.float32) * ATTN_SCALE
        p = jax.nn.softmax(s, axis=-1).astype(v.dtype)
        return jnp.einsum('bhqk,bkhd->bqhd', p, v)

    o = lax.map(one_block, qb)
    return o.transpose(1, 0, 2, 3, 4).reshape(B, L, H * MLA_V)


def mixer_conv_mla(h, hc, cos, sin, w_in, conv_w, q_norm_g, w_uq, kv_norm_g, w_ukv, w_out):
    B, L, _ = h.shape
    p = h @ w_in
    c0 = 3 * CONV_DIM
    gate_b, gate_c, hv = jnp.split(p[..., :c0], 3, axis=-1)
    y_conv = gate_b * dwconv3(gate_c * hv, conv_w)
    q_c = p[..., c0:c0 + MLA_Q_RANK]
    kv_c = p[..., c0 + MLA_Q_RANK:c0 + MLA_Q_RANK + MLA_KV_RANK]
    k_r = p[..., c0 + MLA_Q_RANK + MLA_KV_RANK:]
    q = (rmsnorm(q_c, q_norm_g) @ w_uq).reshape(B, L, MLA_HEADS, MLA_QK)
    q = jnp.concatenate([q[..., :MLA_NOPE], rope_2d(q[..., MLA_NOPE:], cos, sin)], axis=-1)
    k_lat, v_lat = mla_keys_values(kv_c, k_r, kv_norm_g, w_ukv, (cos, sin))
    pc = hc @ w_in[:, c0 + MLA_Q_RANK:]
    k_ctx, v_ctx = mla_keys_values(pc[..., :MLA_KV_RANK], pc[..., MLA_KV_RANK:], kv_norm_g, w_ukv)
    k = jnp.concatenate([k_ctx, k_lat], axis=1)
    v = jnp.concatenate([v_ctx, v_lat], axis=1)
    y_att = block_attention(q, k, v)
    return jnp.concatenate([y_conv, y_att], axis=-1) @ w_out


def hyena_filters(L, w1, b1, f1, w2, b2, f2, w3):
    f32 = jnp.float32
    t = jnp.linspace(0.0, 1.0, L, dtype=f32)[:, None]
    bands = jnp.linspace(1e-4, HY_BANDS - 1, HY_BANDS, dtype=f32)
    w = (2.0 * math.pi / L) * jnp.arange(L, dtype=f32)[:, None]
    z = jnp.concatenate([t, jnp.cos(bands * w), -jnp.sin(bands * w)], axis=-1)
    a = jnp.sin(f1.astype(f32) * (z @ w1.astype(f32) + b1.astype(f32)))
    a = jnp.sin(f2.astype(f32) * (a @ w2.astype(f32) + b2.astype(f32)))
    hf = a @ w3.astype(f32)
    deltas = jnp.abs(jnp.linspace(math.log(HY_TARGET) / HY_SLOW, math.log(HY_TARGET) / HY_FAST, HY_DIM, dtype=f32))
    decay = jnp.exp(-t * deltas)
    h_fwd = hf[:, :HY_DIM] * decay
    h_bwd = hf[:, HY_DIM:] * decay
    k = jnp.concatenate([h_fwd, jnp.zeros((1, HY_DIM), f32), h_bwd[:0:-1]], axis=0)
    return k / (jnp.sum(jnp.abs(k), axis=0, keepdims=True) + 1e-6)


def fft_long_conv(u, k):
    L = u.shape[1]
    uf = jnp.fft.rfft(u.astype(jnp.float32), n=2 * L, axis=1)
    kf = jnp.fft.rfft(k, n=2 * L, axis=0)
    y = jnp.fft.irfft(uf * kf[None], n=2 * L, axis=1)[:, :L]
    return y.astype(u.dtype)


def mixer_gmlp_hyena(h, w_in, gm_norm_g, gm_ws, gm_b, hy_conv_w, hy_conv_b, hy_w1, hy_b1, hy_f1,
                     hy_w2, hy_b2, hy_f2, hy_w3, hy_bias, w_out):
    B, L, _ = h.shape
    p = h @ w_in
    z = jax.nn.gelu(p[..., :2 * GM_DIM])
    u, v = jnp.split(z, 2, axis=-1)
    v = rmsnorm(v, gm_norm_g).reshape(B, L // GM_CHUNK, GM_CHUNK, GM_GROUPS, GM_DIM // GM_GROUPS)
    s = jnp.einsum('gpq,bnqgc->bnpgc', gm_ws, v) + gm_b.T[:, :, None]
    y_gm = u * s.reshape(B, L, GM_DIM)
    xs = dwconv3(p[..., 2 * GM_DIM:], hy_conv_w, hy_conv_b)
    x0, x1, hv = jnp.split(xs, 3, axis=-1)
    filt = hyena_filters(L, hy_w1, hy_b1, hy_f1, hy_w2, hy_b2, hy_f2, hy_w3)
    g1 = x1 * hv
    y_hy = x0 * (fft_long_conv(g1, filt) + hy_bias * g1)
    return jnp.concatenate([y_gm, y_hy], axis=-1) @ w_out


def peer(h, w_q, sub_keys, u_tab, v_tab):
    B, L, D = h.shape
    T = B * L
    hf = h.reshape(T, D)
    q = (hf @ w_q).reshape(T, PEER_HEADS, 2, PEER_HALF)
    s = jnp.einsum('thpd,pnd->thpn', q, sub_keys, preferred_element_type=jnp.float32)
    s_top, i_top = lax.top_k(s, PEER_TOPK)
    n_cand = PEER_TOPK * PEER_TOPK
    cand_s = (s_top[:, :, 0, :, None] + s_top[:, :, 1, None, :]).reshape(T, PEER_HEADS, n_cand)
    cand_i = (i_top[:, :, 0, :, None] * PEER_KEYS + i_top[:, :, 1, None, :]).reshape(T, PEER_HEADS, n_cand)
    best_s, best_pos = lax.top_k(cand_s, PEER_TOPK)
    idx = jnp.take_along_axis(cand_i, best_pos, axis=-1)
    gate = jax.nn.softmax(best_s, axis=-1)
    nb = T // PEER_BLOCK

    def one_block(args):
        xb, ib, gb = args
        a = jax.nn.gelu(jnp.einsum('phkd,pd->phk', u_tab[ib], xb, preferred_element_type=jnp.float32))
        return jnp.einsum('phk,phkd->pd', (gb * a).astype(xb.dtype), v_tab[ib])

    out = lax.map(one_block, (hf.reshape(nb, PEER_BLOCK, D),
                              idx.reshape(nb, PEER_BLOCK, PEER_HEADS, PEER_TOPK),
                              gate.reshape(nb, PEER_BLOCK, PEER_HEADS, PEER_TOPK)))
    return out.reshape(B, L, D)


def setup_inputs(seed: int = 0) -> dict:
    key = jax.random.key(seed)
    ks = iter(jax.random.split(key, 64))

    def nrm(shape, std):
        return jax.random.normal(next(ks), shape, jnp.float32) * std

    def gain(n):
        return 1.0 + nrm((n,), 0.02)

    D = D_MODEL
    out = {}
    out['x'] = nrm((BATCH, SEQ, D), 1.0)
    out['c'] = nrm((BATCH, D), 1.0)
    out['ctx'] = nrm((BATCH, CTX_LEN, D), 1.0)
    out['c_ctx'] = nrm((D,), 1.0)
    out['ln1_g0'] = gain(D)
    out['w_mod0'] = nrm((D, N_MOD * D), 0.5 * D ** -0.5)
    out['b_mod0'] = nrm((N_MOD * D,), 0.01)
    out['w_in0'] = nrm((D, IN0_COLS), D ** -0.5)
    out['conv_w0'] = nrm((3, CONV_DIM), 3 ** -0.5)
    out['q_norm_g0'] = gain(MLA_Q_RANK)
    out['w_uq0'] = nrm((MLA_Q_RANK, MLA_HEADS * MLA_QK), MLA_Q_RANK ** -0.5)
    out['kv_norm_g0'] = gain(MLA_KV_RANK)
    out['w_ukv0'] = nrm((MLA_KV_RANK, MLA_HEADS * (MLA_NOPE + MLA_V)), MLA_KV_RANK ** -0.5)
    out['w_out0'] = nrm((MIX0_WIDTH, D), MIX0_WIDTH ** -0.5)
    out['ln2_g0'] = gain(D)
    out['peer_wq0'] = nrm((D, PEER_HEADS * PEER_QDIM), D ** -0.5)
    out['peer_keys0'] = nrm((2, PEER_KEYS, PEER_HALF), PEER_HALF ** -0.5)
    out['peer_u0'] = nrm((PEER_EXPERTS, D), D ** -0.5)
    out['peer_v0'] = nrm((PEER_EXPERTS, D), 1.0)
    out['ln1_g1'] = gain(D)
    out['w_mod1'] = nrm((D, N_MOD * D), 0.5 * D ** -0.5)
    out['b_mod1'] = nrm((N_MOD * D,), 0.01)
    out['w_in1'] = nrm((D, IN1_COLS), D ** -0.5)
    out['gm_norm_g1'] = gain(GM_DIM)
    out['gm_ws1'] = nrm((GM_GROUPS, GM_CHUNK, GM_CHUNK), GM_CHUNK ** -0.5)
    out['gm_b1'] = 1.0 + nrm((GM_GROUPS, GM_CHUNK), 0.1)
    out['hy_conv_w1'] = nrm((3, 3 * HY_DIM), 3 ** -0.5)
    out['hy_conv_b1'] = nrm((3 * HY_DIM,), 0.01)
    out['hy_w1'] = nrm((HY_EMB, HY_FFN), HY_EMB ** -0.5)
    out['hy_b1'] = nrm((HY_FFN,), 0.01)
    out['hy_f1'] = 1.0 + nrm((HY_FFN,), 0.1)
    out['hy_w2'] = nrm((HY_FFN, HY_FFN), HY_FFN ** -0.5)
    out['hy_b2'] = nrm((HY_FFN,), 0.01)
    out['hy_f2'] = 1.0 + nrm((HY_FFN,), 0.1)
    out['hy_w3'] = nrm((HY_FFN, 2 * HY_DIM), HY_FFN ** -0.5)
    out['hy_bias1'] = nrm((HY_DIM,), 0.1)
    out['w_out1'] = nrm((MIX1_WIDTH, D), MIX1_WIDTH ** -0.5)
    out['ln2_g1'] = gain(D)
    out['peer_wq1'] = nrm((D, PEER_HEADS * PEER_QDIM), D ** -0.5)
    out['peer_keys1'] = nrm((2, PEER_KEYS, PEER_HALF), PEER_HALF ** -0.5)
    out['peer_u1'] = nrm((PEER_EXPERTS, D), D ** -0.5)
    out['peer_v1'] = nrm((PEER_EXPERTS, D), 1.0)
    out['final_g'] = gain(D)
    return out


def reference(x, c, ctx, c_ctx,
              ln1_g0, w_mod0, b_mod0, w_in0, conv_w0, q_norm_g0, w_uq0, kv_norm_g0, w_ukv0, w_out0,
              ln2_g0, peer_wq0, peer_keys0, peer_u0, peer_v0,
              ln1_g1, w_mod1, b_mod1, w_in1, gm_norm_g1, gm_ws1, gm_b1, hy_conv_w1, hy_conv_b1,
              hy_w1, hy_b1, hy_f1, hy_w2, hy_b2, hy_f2, hy_w3, hy_bias1, w_out1,
              ln2_g1, peer_wq1, peer_keys1, peer_u1, peer_v1, final_g):
    D = D_MODEL
    n_rows = x.shape[1] // GRID_W
    cos, sin = axial_rope_tables(n_rows)
    ln1 = (ln1_g0, ln1_g1)
    ln2 = (ln2_g0, ln2_g1)
    w_mod = (w_mod0, w_mod1)
    b_mod = (b_mod0, b_mod1)
    mixers = ((w_in0, conv_w0, q_norm_g0, w_uq0, kv_norm_g0, w_ukv0, w_out0),
              (w_in1, gm_norm_g1, gm_ws1, gm_b1, hy_conv_w1, hy_conv_b1, hy_w1, hy_b1, hy_f1,
               hy_w2, hy_b2, hy_f2, hy_w3, hy_bias1, w_out1))
    peers = ((peer_wq0, peer_keys0, peer_u0, peer_v0),
             (peer_wq1, peer_keys1, peer_u1, peer_v1))
    for i in range(DEPTH):
        mod = (jax.nn.silu(c) @ w_mod[i] + b_mod[i]).reshape(c.shape[0], N_MOD, 1, D)
        sh_m, sc_m, g_m, sh_f, sc_f, g_f = [mod[:, j] for j in range(N_MOD)]
        h = modulate(rmsnorm(x, ln1[i]), sh_m, sc_m)
        if i % 2 == 0:
            mc = jax.nn.silu(c_ctx) @ w_mod[i][:, :2 * D] + b_mod[i][:2 * D]
            hc = modulate(rmsnorm(ctx, ln1[i]), mc[:D], mc[D:])
            y = mixer_conv_mla(h, hc, cos, sin, *mixers[i])
        else:
            y = mixer_gmlp_hyena(h, *mixers[i])
        x = x + g_m * y
        x = x + g_f * peer(modulate(rmsnorm(x, ln2[i]), sh_f, sc_f), *peers[i])
    return rmsnorm(x, final_g)
```

```python
import functools
import math

import jax
import jax.numpy as jnp
from jax import lax
from jax.experimental import pallas as pl
from jax.experimental.pallas import tpu as pltpu

F32 = jnp.float32
BF16 = jnp.bfloat16

D_MODEL = 2048
GRID_W = 64
N_MOD = 6
NORM_EPS = 1e-6
CONV_DIM = D_MODEL // 2
MLA_HEADS = 8
MLA_NOPE = 128
MLA_ROPE = 64
MLA_V = 128
MLA_QK = MLA_NOPE + MLA_ROPE
MLA_Q_RANK = D_MODEL // 4
MLA_KV_RANK = D_MODEL // 8
ATTN_SCALE = MLA_QK ** -0.5
ROPE_AXIS = MLA_ROPE // 2
ROPE_THETA = 10000.0
GM_DIM = D_MODEL // 2
GM_GROUPS = 8
GM_CHUNK = 128
HY_DIM = D_MODEL // 2
HY_BANDS = 16
HY_TARGET = 1e-2
HY_FAST = 0.3
HY_SLOW = 1.5
PEER_KEYS = 128
PEER_HEADS = 8
PEER_TOPK = 16
PEER_QDIM = 256
PEER_HALF = PEER_QDIM // 2

LANES = 128
VMEM_LIMIT = 56 * 1024 * 1024


def _cparams(sem):
    return pltpu.CompilerParams(dimension_semantics=sem, vmem_limit_bytes=VMEM_LIMIT)


def _tile(n, target, mult=LANES):
    if n <= target:
        return n
    best = None
    for t in range(mult, target + 1, mult):
        if n % t == 0:
            best = t
    assert best is not None, (n, target, mult)
    return best


def _modvec_kernel(a_ref, w_ref, b_ref, o_ref):
    a = a_ref[...]
    a = a * jax.nn.sigmoid(a)
    o_ref[...] = jnp.dot(a, w_ref[...], preferred_element_type=F32,
                         precision=lax.Precision.HIGHEST) + b_ref[...]


def _mod_vectors(rows, w_mod, b_mod):
    d, n = w_mod.shape
    tn = _tile(n, 1024)
    return pl.pallas_call(
        _modvec_kernel,
        grid=(n // tn,),
        in_specs=[pl.BlockSpec((8, d), lambda j: (0, 0)),
                  pl.BlockSpec((d, tn), lambda j: (0, j)),
                  pl.BlockSpec((1, tn), lambda j: (0, j))],
        out_specs=pl.BlockSpec((8, tn), lambda j: (0, j)),
        out_shape=jax.ShapeDtypeStruct((8, n), F32),
        compiler_params=_cparams(("arbitrary",)),
        name="mod_vectors",
    )(rows, w_mod, b_mod.reshape(1, n))


def _nmm_kernel(x_ref, g_ref, sh_ref, sc_ref, w_ref, *rest, emit_h):
    if emit_h:
        o_ref, ho_ref, h_ref = rest
    else:
        o_ref, h_ref = rest

    @pl.when(pl.program_id(1) == 0)
    def _():
        xf = x_ref[...].astype(F32)
        ms = jnp.mean(xf * xf, axis=-1, keepdims=True)
        y = xf * lax.rsqrt(ms + NORM_EPS) * g_ref[...]
        h = (y * (1.0 + sc_ref[...]) + sh_ref[...]).astype(BF16)
        h_ref[...] = h
        if emit_h:
            ho_ref[...] = h

    o_ref[...] = jnp.dot(h_ref[...], w_ref[...], preferred_element_type=F32).astype(o_ref.dtype)


def _norm_mod_matmul(x, g, shift, scale, w, *, out_dtype=F32, tm=512, tn=512, emit_h=False):
    m, k = x.shape
    n = w.shape[1]
    tm = _tile(m, tm, 8)
    tn = _tile(n, tn)
    row = lambda v: v.reshape(1, k).astype(F32)
    out_shape = [jax.ShapeDtypeStruct((m, n), out_dtype)]
    out_specs = [pl.BlockSpec((tm, tn), lambda i, j: (i, j))]
    if emit_h:
        out_shape.append(jax.ShapeDtypeStruct((m, k), BF16))
        out_specs.append(pl.BlockSpec((tm, k), lambda i, j: (i, 0)))
    res = pl.pallas_call(
        functools.partial(_nmm_kernel, emit_h=emit_h),
        grid=(m // tm, n // tn),
        in_specs=[pl.BlockSpec((tm, k), lambda i, j: (i, 0)),
                  pl.BlockSpec((1, k), lambda i, j: (0, 0)),
                  pl.BlockSpec((1, k), lambda i, j: (0, 0)),
                  pl.BlockSpec((1, k), lambda i, j: (0, 0)),
                  pl.BlockSpec((k, tn), lambda i, j: (0, j))],
        out_specs=out_specs,
        out_shape=out_shape,
        scratch_shapes=[pltpu.VMEM((tm, k), BF16)],
        compiler_params=_cparams(("parallel", "arbitrary")),
        name="norm_mod_matmul",
    )(x, row(g), row(shift), row(scale), w)
    return res if emit_h else res[0]


def _mmres_kernel(y_ref, w_ref, x_ref, gate_ref, o_ref):
    o_ref[...] = x_ref[...] + gate_ref[...] * jnp.dot(
        y_ref[...], w_ref[...], preferred_element_type=F32)


def _matmul_residual(y, w, x, gate, *, tm=512, tn=512):
    m, k = y.shape
    n = w.shape[1]
    tm = _tile(m, tm, 8)
    tn = _tile(n, tn)
    return pl.pallas_call(
        _mmres_kernel,
        grid=(m // tm, n // tn),
        in_specs=[pl.BlockSpec((tm, k), lambda i, j: (i, 0)),
                  pl.BlockSpec((k, tn), lambda i, j: (0, j)),
                  pl.BlockSpec((tm, tn), lambda i, j: (i, j)),
                  pl.BlockSpec((1, tn), lambda i, j: (0, j))],
        out_specs=pl.BlockSpec((tm, tn), lambda i, j: (i, j)),
        out_shape=jax.ShapeDtypeStruct((m, n), F32),
        compiler_params=_cparams(("parallel", "parallel")),
        name="matmul_residual",
    )(y, w, x, gate.reshape(1, n).astype(F32))


def _flash_kernel(q_ref, k_ref, v_ref, o_ref, m_ref, l_ref, acc_ref):
    kv = pl.program_id(2)

    @pl.when(kv == 0)
    def _():
        m_ref[...] = jnp.full(m_ref.shape, -jnp.inf, F32)
        l_ref[...] = jnp.zeros(l_ref.shape, F32)
        acc_ref[...] = jnp.zeros(acc_ref.shape, F32)

    s = lax.dot_general(q_ref[0], k_ref[0], (((1,), (1,)), ((), ())),
                        preferred_element_type=F32)
    m_prev = m_ref[...]
    m_new = jnp.maximum(m_prev, jnp.max(s, axis=-1, keepdims=True))
    alpha = jnp.exp(m_prev - m_new)
    p = jnp.exp(s - m_new)
    l_ref[...] = alpha * l_ref[...] + jnp.sum(p, axis=-1, keepdims=True)
    acc_ref[...] = alpha * acc_ref[...] + jnp.dot(
        p.astype(BF16), v_ref[0], preferred_element_type=F32)
    m_ref[...] = m_new

    @pl.when(kv == pl.num_programs(2) - 1)
    def _():
        o_ref[...] = (acc_ref[...] / l_ref[...]).astype(o_ref.dtype)


def _attention(q, k, v, *, tq=512, tk=2048):
    h, l, dqk = q.shape
    lk = k.shape[1]
    dv = v.shape[2]
    tq = _tile(l, tq)
    tk = _tile(lk, tk)
    return pl.pallas_call(
        _flash_kernel,
        grid=(h, l // tq, lk // tk),
        in_specs=[pl.BlockSpec((1, tq, dqk), lambda hh, i, j: (hh, i, 0)),
                  pl.BlockSpec((1, tk, dqk), lambda hh, i, j: (hh, j, 0)),
                  pl.BlockSpec((1, tk, dv), lambda hh, i, j: (hh, j, 0))],
        out_specs=pl.BlockSpec((tq, dv), lambda hh, i, j: (i, hh)),
        out_shape=jax.ShapeDtypeStruct((l, h * dv), BF16),
        scratch_shapes=[pltpu.VMEM((tq, 1), F32), pltpu.VMEM((tq, 1), F32),
                        pltpu.VMEM((tq, dv), F32)],
        compiler_params=_cparams(("parallel", "parallel", "arbitrary")),
        name="mla_attention",
    )(q, k, v)


def _peer_kernel(h_ref, thr_ref, c_ref, s2_ref, p2_ref, u_ref, v_ref, x_ref, gf_ref, fg_ref,
                 o_ref, acc_ref, *, n_e1, final_norm):
    j = pl.program_id(1)

    @pl.when(j == 0)
    def _():
        acc_ref[...] = jnp.zeros(acc_ref.shape, F32)

    a_t = lax.dot_general(u_ref[...], h_ref[...], (((1,), (1,)), ((), ())),
                          preferred_element_type=F32)
    blocks = []
    for r in range(n_e1):
        e1 = j * n_e1 + r
        w = None
        for hd in range(PEER_HEADS):
            thr_row = thr_ref[hd, pl.ds(e1, 1), :]
            c_row = c_ref[hd, pl.ds(e1, 1), :]
            term = jnp.where(s2_ref[hd] >= thr_row, p2_ref[hd], 0.0) * c_row
            w = term if w is None else w + term
        blocks.append(w)
    w_t = blocks[0] if n_e1 == 1 else jnp.concatenate(blocks, axis=0)
    wa_t = w_t * jax.nn.gelu(a_t)
    wa = wa_t.T.astype(BF16)
    acc_ref[...] += jnp.dot(wa, v_ref[...], preferred_element_type=F32)

    @pl.when(j == pl.num_programs(1) - 1)
    def _():
        y = x_ref[...] + gf_ref[...] * acc_ref[...]
        if final_norm:
            ms = jnp.mean(y * y, axis=-1, keepdims=True)
            y = y * lax.rsqrt(ms + NORM_EPS) * fg_ref[...]
        o_ref[...] = y


def _peer_experts(h, thr_t, c_t, s2_t, p2_t, u_tab, v_tab, x, g_f, final_g, *, final_norm,
                  tq=512, te=512):
    t, d = h.shape
    e = u_tab.shape[0]
    tq = _tile(t, tq)
    n_e1 = te // PEER_KEYS
    sel_spec = pl.BlockSpec((PEER_HEADS, PEER_KEYS, tq), lambda i, j: (0, 0, i))
    return pl.pallas_call(
        functools.partial(_peer_kernel, n_e1=n_e1, final_norm=final_norm),
        grid=(t // tq, e // te),
        in_specs=[pl.BlockSpec((tq, d), lambda i, j: (i, 0)),
                  sel_spec, sel_spec, sel_spec, sel_spec,
                  pl.BlockSpec((te, d), lambda i, j: (j, 0)),
                  pl.BlockSpec((te, d), lambda i, j: (j, 0)),
                  pl.BlockSpec((tq, d), lambda i, j: (i, 0)),
                  pl.BlockSpec((1, d), lambda i, j: (0, 0)),
                  pl.BlockSpec((1, d), lambda i, j: (0, 0))],
        out_specs=pl.BlockSpec((tq, d), lambda i, j: (i, 0)),
        out_shape=jax.ShapeDtypeStruct((t, d), F32),
        scratch_shapes=[pltpu.VMEM((tq, d), F32)],
        compiler_params=_cparams(("parallel", "arbitrary")),
        name="peer_experts",
    )(h, thr_t, c_t, s2_t, p2_t, u_tab, v_tab, x, g_f.reshape(1, d).astype(F32),
      final_g.reshape(1, d).astype(F32))


def _peer_selection(q, sub_keys):
    t = q.shape[0]
    qh = q.reshape(t, PEER_HEADS, 2, PEER_HALF)
    s = jnp.einsum('thpd,pnd->thpn', qh, sub_keys, preferred_element_type=F32)
    s_top, _ = lax.top_k(s, PEER_TOPK)
    cand = (s_top[:, :, 0, :, None] + s_top[:, :, 1, None, :]).reshape(
        t, PEER_HEADS, PEER_TOPK * PEER_TOPK)
    best, _ = lax.top_k(cand, PEER_TOPK + 1)
    tau = 0.5 * (best[..., PEER_TOPK - 1] + best[..., PEER_TOPK])
    z = jnp.sum(jnp.exp(best[..., :PEER_TOPK] - best[..., :1]), axis=-1)
    s1, s2 = s[:, :, 0, :], s[:, :, 1, :]
    m1 = s1 >= s_top[:, :, 0, PEER_TOPK - 1:]
    m2 = s2 >= s_top[:, :, 1, PEER_TOPK - 1:]
    c = jnp.where(m1, jnp.exp(s1 - s_top[:, :, 0, :1]), 0.0) / z[..., None]
    p2 = jnp.where(m2, jnp.exp(s2 - s_top[:, :, 1, :1]), 0.0)
    thr = tau[..., None] - s1
    tr = lambda a: jnp.transpose(a, (1, 2, 0))
    return tr(thr), tr(c), tr(s2), tr(p2)


def _peer(x, g, shift, scale, gate, w_q, sub_keys, u_tab, v_tab, final_g, final_norm):
    q, h = _norm_mod_matmul(x, g, shift, scale, w_q.astype(BF16), emit_h=True)
    thr_t, c_t, s2_t, p2_t = _peer_selection(q, sub_keys)
    return _peer_experts(h, thr_t, c_t, s2_t, p2_t, u_tab.astype(BF16), v_tab.astype(BF16),
                         x, gate, final_g, final_norm=final_norm)


def _dwconv3(u, w, b=None):
    up = jnp.pad(u, ((1, 1), (0, 0)))
    y = up[:-2] * w[0] + up[1:-1] * w[1] + up[2:] * w[2]
    return y if b is None else y + b


def _rope_tables(n_rows):
    rr, cc = jnp.meshgrid(jnp.arange(n_rows), jnp.arange(GRID_W), indexing='ij')
    inv = ROPE_THETA ** (-jnp.arange(ROPE_AXIS // 2, dtype=F32) / (ROPE_AXIS // 2))
    ar = rr.reshape(-1, 1).astype(F32) * inv
    ac = cc.reshape(-1, 1).astype(F32) * inv
    ang = jnp.concatenate([ar, ar, ac, ac], axis=-1)
    return jnp.cos(ang), jnp.sin(ang)


def _rope(u, cos, sin):
    us = u.reshape(u.shape[:-1] + (2, 2, ROPE_AXIS // 2))
    rot = jnp.stack([-us[..., 1, :], us[..., 0, :]], axis=-2).reshape(u.shape)
    return u * cos[:, None, :] + rot * sin[:, None, :]


def _mixer_conv_mla(x, ctx, g, sh, sc, mc, gate, cos, sin, w_in, conv_w, q_norm_g, w_uq,
                    kv_norm_g, w_ukv, w_out):
    l, d = x.shape
    c0 = 3 * CONV_DIM
    n_in = w_in.shape[1]
    n_pad = -n_in % 512
    w_in_b = jnp.pad(w_in, ((0, 0), (0, n_pad))).astype(BF16)
    p = _norm_mod_matmul(x, g, sh, sc, w_in_b)
    y_conv = p[:, :CONV_DIM] * _dwconv3(p[:, CONV_DIM:2 * CONV_DIM] * p[:, 2 * CONV_DIM:c0], conv_w)
    q_c = p[:, c0:c0 + MLA_Q_RANK]
    kv_c = p[:, c0 + MLA_Q_RANK:c0 + MLA_Q_RANK + MLA_KV_RANK]
    k_r = p[:, c0 + MLA_Q_RANK + MLA_KV_RANK:n_in]
    zq = jnp.zeros((MLA_Q_RANK,), F32)
    zkv = jnp.zeros((MLA_KV_RANK,), F32)
    w_ukv_b = w_ukv.astype(BF16)
    q = _norm_mod_matmul(q_c, q_norm_g, zq, zq, w_uq.astype(BF16)).reshape(l, MLA_HEADS, MLA_QK)
    q = jnp.concatenate([q[..., :MLA_NOPE], _rope(q[..., MLA_NOPE:], cos, sin)], axis=-1)
    q = jnp.transpose(q * ATTN_SCALE, (1, 0, 2)).astype(BF16)
    kv = _norm_mod_matmul(kv_c, kv_norm_g, zkv, zkv, w_ukv_b).reshape(l, MLA_HEADS, MLA_NOPE + MLA_V)
    k_r = _rope(k_r[:, None, :], cos, sin)
    n_c = n_in - (c0 + MLA_Q_RANK)
    w_ctx = jnp.pad(w_in[:, c0 + MLA_Q_RANK:], ((0, 0), (0, -n_c % LANES))).astype(BF16)
    pc = _norm_mod_matmul(ctx, g, mc[:d], mc[d:], w_ctx)
    lc = ctx.shape[0]
    kv_x = _norm_mod_matmul(pc[:, :MLA_KV_RANK], kv_norm_g, zkv, zkv, w_ukv_b).reshape(
        lc, MLA_HEADS, MLA_NOPE + MLA_V)
    k_rx = pc[:, MLA_KV_RANK:n_c][:, None, :]

    def keys(kvm, kr):
        return jnp.concatenate(
            [kvm[..., :MLA_NOPE], jnp.broadcast_to(kr, kvm.shape[:2] + (MLA_ROPE,))], axis=-1)

    k = jnp.concatenate([keys(kv_x, k_rx), keys(kv, k_r)], axis=0)
    v = jnp.concatenate([kv_x[..., MLA_NOPE:], kv[..., MLA_NOPE:]], axis=0)
    k = jnp.transpose(k, (1, 0, 2)).astype(BF16)
    v = jnp.transpose(v, (1, 0, 2)).astype(BF16)
    y_att = _attention(q, k, v)
    y = jnp.concatenate([y_conv.astype(BF16), y_att], axis=-1)
    return _matmul_residual(y, w_out.astype(BF16), x, gate)


def _hyena_filters(l, w1, b1, f1, w2, b2, f2, w3):
    hi = lax.Precision.HIGHEST
    t = jnp.linspace(0.0, 1.0, l, dtype=F32)[:, None]
    bands = jnp.linspace(1e-4, HY_BANDS - 1, HY_BANDS, dtype=F32)
    w = (2.0 * math.pi / l) * jnp.arange(l, dtype=F32)[:, None]
    z = jnp.concatenate([t, jnp.cos(bands * w), -jnp.sin(bands * w)], axis=-1)
    a = jnp.sin(f1 * (jnp.dot(z, w1, precision=hi) + b1))
    a = jnp.sin(f2 * (jnp.dot(a, w2, precision=hi) + b2))
    hf = jnp.dot(a, w3, precision=hi)
    deltas = jnp.abs(jnp.linspace(math.log(HY_TARGET) / HY_SLOW, math.log(HY_TARGET) / HY_FAST,
                                  HY_DIM, dtype=F32))
    decay = jnp.exp(-t * deltas)
    h_fwd = hf[:, :HY_DIM] * decay
    h_bwd = hf[:, HY_DIM:] * decay
    k = jnp.concatenate([h_fwd, jnp.zeros((1, HY_DIM), F32), h_bwd[:0:-1]], axis=0)
    return k / (jnp.sum(jnp.abs(k), axis=0, keepdims=True) + 1e-6)


def _fft_long_conv(u, k):
    l = u.shape[0]
    uf = jnp.fft.rfft(u, n=2 * l, axis=0)
    kf = jnp.fft.rfft(k, n=2 * l, axis=0)
    return jnp.fft.irfft(uf * kf, n=2 * l, axis=0)[:l]


def _mixer_gmlp_hyena(x, g, sh, sc, gate, w_in, gm_norm_g, gm_ws, gm_b, hy_conv_w, hy_conv_b,
                      hy_w1, hy_b1, hy_f1, hy_w2, hy_b2, hy_f2, hy_w3, hy_bias, w_out):
    l, d = x.shape
    p = _norm_mod_matmul(x, g, sh, sc, w_in.astype(BF16))
    z = jax.nn.gelu(p[:, :2 * GM_DIM])
    u, v = z[:, :GM_DIM], z[:, GM_DIM:]
    v = v * lax.rsqrt(jnp.mean(v * v, axis=-1, keepdims=True) + NORM_EPS) * gm_norm_g
    v = v.reshape(l // GM_CHUNK, GM_CHUNK, GM_GROUPS, GM_DIM // GM_GROUPS)
    s = jnp.einsum('gpq,nqgc->npgc', gm_ws, v) + gm_b.T[:, :, None]
    y_gm = u * s.reshape(l, GM_DIM)
    xs = _dwconv3(p[:, 2 * GM_DIM:], hy_conv_w, hy_conv_b)
    x0, x1, hv = xs[:, :HY_DIM], xs[:, HY_DIM:2 * HY_DIM], xs[:, 2 * HY_DIM:]
    filt = _hyena_filters(l, hy_w1, hy_b1, hy_f1, hy_w2, hy_b2, hy_f2, hy_w3)
    g1 = x1 * hv
    y_hy = x0 * (_fft_long_conv(g1, filt) + hy_bias * g1)
    y = jnp.concatenate([y_gm, y_hy], axis=-1).astype(BF16)
    return _matmul_residual(y, w_out.astype(BF16), x, gate)


def kernel(x, c, ctx, c_ctx, ln1_g0, w_mod0, b_mod0, w_in0, conv_w0, q_norm_g0, w_uq0, kv_norm_g0, w_ukv0, w_out0, ln2_g0, peer_wq0, peer_keys0, peer_u0, peer_v0, ln1_g1, w_mod1, b_mod1, w_in1, gm_norm_g1, gm_ws1, gm_b1, hy_conv_w1, hy_conv_b1, hy_w1, hy_b1, hy_f1, hy_w2, hy_b2, hy_f2, hy_w3, hy_bias1, w_out1, ln2_g1, peer_wq1, peer_keys1, peer_u1, peer_v1, final_g):
    b, l, d = x.shape
    assert b == 1 and d == D_MODEL
    xt = x.reshape(l, d)
    cos, sin = _rope_tables(l // GRID_W)
    rows = jnp.zeros((8, d), F32).at[0].set(c[0]).at[1].set(c_ctx)

    mod0 = _mod_vectors(rows, w_mod0, b_mod0)
    sh_m, sc_m, g_m, sh_f, sc_f, g_f = [mod0[0, j * d:(j + 1) * d] for j in range(N_MOD)]
    mc = mod0[1, :2 * d]
    xt = _mixer_conv_mla(xt, ctx[0], ln1_g0, sh_m, sc_m, mc, g_m, cos, sin, w_in0, conv_w0,
                         q_norm_g0, w_uq0, kv_norm_g0, w_ukv0, w_out0)
    xt = _peer(xt, ln2_g0, sh_f, sc_f, g_f, peer_wq0, peer_keys0, peer_u0, peer_v0, final_g,
               final_norm=False)

    mod1 = _mod_vectors(rows, w_mod1, b_mod1)
    sh_m, sc_m, g_m, sh_f, sc_f, g_f = [mod1[0, j * d:(j + 1) * d] for j in range(N_MOD)]
    xt = _mixer_gmlp_hyena(xt, ln1_g1, sh_m, sc_m, g_m, w_in1, gm_norm_g1, gm_ws1, gm_b1,
                           hy_conv_w1, hy_conv_b1, hy_w1, hy_b1, hy_f1, hy_w2, hy_b2, hy_f2,
                           hy_w3, hy_bias1, w_out1)
    xt = _peer(xt, ln2_g1, sh_f, sc_f, g_f, peer_wq1, peer_keys1, peer_u1, peer_v1, final_g,
               final_norm=True)
    return xt.reshape(b, l, d)
```

```python
import functools
import math

import jax
import jax.numpy as jnp
from jax import lax
from jax.experimental import pallas as pl
from jax.experimental.pallas import tpu as pltpu

F32 = jnp.float32
BF16 = jnp.bfloat16

D_MODEL = 2048
GRID_W = 64
N_MOD = 6
NORM_EPS = 1e-6
CONV_DIM = D_MODEL // 2
MLA_HEADS = 8
MLA_NOPE = 128
MLA_ROPE = 64
MLA_V = 128
MLA_QK = MLA_NOPE + MLA_ROPE
MLA_Q_RANK = D_MODEL // 4
MLA_KV_RANK = D_MODEL // 8
ATTN_SCALE = MLA_QK ** -0.5
ROPE_AXIS = MLA_ROPE // 2
ROPE_THETA = 10000.0
GM_DIM = D_MODEL // 2
GM_GROUPS = 8
GM_CHUNK = 128
HY_DIM = D_MODEL // 2
HY_BANDS = 16
HY_TARGET = 1e-2
HY_FAST = 0.3
HY_SLOW = 1.5
PEER_KEYS = 128
PEER_HEADS = 8
PEER_TOPK = 16
PEER_QDIM = 256
PEER_HALF = PEER_QDIM // 2

LANES = 128
VMEM_LIMIT = 56 * 1024 * 1024


def _cparams(sem):
    return pltpu.CompilerParams(dimension_semantics=sem, vmem_limit_bytes=VMEM_LIMIT)


def _tile(n, target, mult=LANES):
    if n <= target:
        return n
    best = None
    for t in range(mult, target + 1, mult):
        if n % t == 0:
            best = t
    assert best is not None, (n, target, mult)
    return best


def _modvec_kernel(a_ref, w_ref, b_ref, o_ref):
    a = a_ref[...]
    a = a * jax.nn.sigmoid(a)
    o_ref[...] = jnp.dot(a, w_ref[...], preferred_element_type=F32,
                         precision=lax.Precision.HIGHEST) + b_ref[...]


def _mod_vectors(rows, w_mod, b_mod):
    d, n = w_mod.shape
    tn = _tile(n, 1024)
    return pl.pallas_call(
        _modvec_kernel,
        grid=(n // tn,),
        in_specs=[pl.BlockSpec((8, d), lambda j: (0, 0)),
                  pl.BlockSpec((d, tn), lambda j: (0, j)),
                  pl.BlockSpec((1, tn), lambda j: (0, j))],
        out_specs=pl.BlockSpec((8, tn), lambda j: (0, j)),
        out_shape=jax.ShapeDtypeStruct((8, n), F32),
        compiler_params=_cparams(("arbitrary",)),
        name="mod_vectors",
    )(rows, w_mod, b_mod.reshape(1, n))


def _nmm_kernel(x_ref, g_ref, sh_ref, sc_ref, w_ref, *rest, emit_h):
    if emit_h:
        o_ref, ho_ref, h_ref = rest
    else:
        o_ref, h_ref = rest

    @pl.when(pl.program_id(1) == 0)
    def _():
        xf = x_ref[...].astype(F32)
        ms = jnp.mean(xf * xf, axis=-1, keepdims=True)
        y = xf * lax.rsqrt(ms + NORM_EPS) * g_ref[...]
        h = (y * (1.0 + sc_ref[...]) + sh_ref[...]).astype(BF16)
        h_ref[...] = h
        if emit_h:
            ho_ref[...] = h

    o_ref[...] = jnp.dot(h_ref[...], w_ref[...], preferred_element_type=F32).astype(o_ref.dtype)


def _norm_mod_matmul(x, g, shift, scale, w, *, out_dtype=F32, tm=512, tn=512, emit_h=False):
    m, k = x.shape
    n = w.shape[1]
    tm = _tile(m, tm, 8)
    tn = _tile(n, tn)
    row = lambda v: v.reshape(1, k).astype(F32)
    out_shape = [jax.ShapeDtypeStruct((m, n), out_dtype)]
    out_specs = [pl.BlockSpec((tm, tn), lambda i, j: (i, j))]
    if emit_h:
        out_shape.append(jax.ShapeDtypeStruct((m, k), BF16))
        out_specs.append(pl.BlockSpec((tm, k), lambda i, j: (i, 0)))
    res = pl.pallas_call(
        functools.partial(_nmm_kernel, emit_h=emit_h),
        grid=(m // tm, n // tn),
        in_specs=[pl.BlockSpec((tm, k), lambda i, j: (i, 0)),
                  pl.BlockSpec((1, k), lambda i, j: (0, 0)),
                  pl.BlockSpec((1, k), lambda i, j: (0, 0)),
                  pl.BlockSpec((1, k), lambda i, j: (0, 0)),
                  pl.BlockSpec((k, tn), lambda i, j: (0, j))],
        out_specs=out_specs,
        out_shape=out_shape,
        scratch_shapes=[pltpu.VMEM((tm, k), BF16)],
        compiler_params=_cparams(("parallel", "arbitrary")),
        name="norm_mod_matmul",
    )(x, row(g), row(shift), row(scale), w)
    return res if emit_h else res[0]


def _mmres_kernel(y_ref, w_ref, x_ref, gate_ref, o_ref):
    o_ref[...] = x_ref[...] + gate_ref[...] * jnp.dot(
        y_ref[...], w_ref[...], preferred_element_type=F32)


def _matmul_residual(y, w, x, gate, *, tm=512, tn=512):
    m, k = y.shape
    n = w.shape[1]
    tm = _tile(m, tm, 8)
    tn = _tile(n, tn)
    return pl.pallas_call(
        _mmres_kernel,
        grid=(m // tm, n // tn),
        in_specs=[pl.BlockSpec((tm, k), lambda i, j: (i, 0)),
                  pl.BlockSpec((k, tn), lambda i, j: (0, j)),
                  pl.BlockSpec((tm, tn), lambda i, j: (i, j)),
                  pl.BlockSpec((1, tn), lambda i, j: (0, j))],
        out_specs=pl.BlockSpec((tm, tn), lambda i, j: (i, j)),
        out_shape=jax.ShapeDtypeStruct((m, n), F32),
        compiler_params=_cparams(("parallel", "parallel")),
        name="matmul_residual",
    )(y, w, x, gate.reshape(1, n).astype(F32))


def _flash_kernel(q_ref, k_ref, v_ref, o_ref, m_ref, l_ref, acc_ref):
    kv = pl.program_id(2)

    @pl.when(kv == 0)
    def _():
        m_ref[...] = jnp.full(m_ref.shape, -jnp.inf, F32)
        l_ref[...] = jnp.zeros(l_ref.shape, F32)
        acc_ref[...] = jnp.zeros(acc_ref.shape, F32)

    s = lax.dot_general(q_ref[0], k_ref[0], (((1,), (1,)), ((), ())),
                        preferred_element_type=F32)
    m_prev = m_ref[...]
    m_new = jnp.maximum(m_prev, jnp.max(s, axis=-1, keepdims=True))
    alpha = jnp.exp(m_prev - m_new)
    p = jnp.exp(s - m_new)
    l_ref[...] = alpha * l_ref[...] + jnp.sum(p, axis=-1, keepdims=True)
    acc_ref[...] = alpha * acc_ref[...] + jnp.dot(
        p.astype(BF16), v_ref[0], preferred_element_type=F32)
    m_ref[...] = m_new

    @pl.when(kv == pl.num_programs(2) - 1)
    def _():
        o_ref[...] = (acc_ref[...] / l_ref[...]).astype(o_ref.dtype)


def _attention(q, k, v, *, tq=512, tk=2048):
    h, l, dqk = q.shape
    lk = k.shape[1]
    dv = v.shape[2]
    tq = _tile(l, tq)
    tk = _tile(lk, tk)
    return pl.pallas_call(
        _flash_kernel,
        grid=(h, l // tq, lk // tk),
        in_specs=[pl.BlockSpec((1, tq, dqk), lambda hh, i, j: (hh, i, 0)),
                  pl.BlockSpec((1, tk, dqk), lambda hh, i, j: (hh, j, 0)),
                  pl.BlockSpec((1, tk, dv), lambda hh, i, j: (hh, j, 0))],
        out_specs=pl.BlockSpec((tq, dv), lambda hh, i, j: (i, hh)),
        out_shape=jax.ShapeDtypeStruct((l, h * dv), BF16),
        scratch_shapes=[pltpu.VMEM((tq, 1), F32), pltpu.VMEM((tq, 1), F32),
                        pltpu.VMEM((tq, dv), F32)],
        compiler_params=_cparams(("parallel", "parallel", "arbitrary")),
        name="mla_attention",
    )(q, k, v)


def _peer_kernel(h_ref, thr_ref, c_ref, s2_ref, p2_ref, u_ref, v_ref, x_ref, gf_ref, fg_ref,
                 o_ref, acc_ref, *, n_e1, final_norm):
    j = pl.program_id(1)

    @pl.when(j == 0)
    def _():
        acc_ref[...] = jnp.zeros(acc_ref.shape, F32)

    a_t = lax.dot_general(u_ref[...], h_ref[...], (((1,), (1,)), ((), ())),
                          preferred_element_type=F32)
    blocks = []
    for r in range(n_e1):
        e1 = j * n_e1 + r
        w = None
        for hd in range(PEER_HEADS):
            thr_row = thr_ref[hd, pl.ds(e1, 1), :]
            c_row = c_ref[hd, pl.ds(e1, 1), :]
            term = jnp.where(s2_ref[hd] >= thr_row, p2_ref[hd], 0.0) * c_row
            w = term if w is None else w + term
        blocks.append(w)
    w_t = blocks[0] if n_e1 == 1 else jnp.concatenate(blocks, axis=0)
    wa_t = w_t * jax.nn.gelu(a_t)
    wa = wa_t.T.astype(BF16)
    acc_ref[...] += jnp.dot(wa, v_ref[...], preferred_element_type=F32)

    @pl.when(j == pl.num_programs(1) - 1)
    def _():
        y = x_ref[...] + gf_ref[...] * acc_ref[...]
        if final_norm:
            ms = jnp.mean(y * y, axis=-1, keepdims=True)
            y = y * lax.rsqrt(ms + NORM_EPS) * fg_ref[...]
        o_ref[...] = y


def _peer_experts(h, thr_t, c_t, s2_t, p2_t, u_tab, v_tab, x, g_f, final_g, *, final_norm,
                  tq=512, te=512):
    t, d = h.shape
    e = u_tab.shape[0]
    tq = _tile(t, tq)
    n_e1 = te // PEER_KEYS
    sel_spec = pl.BlockSpec((PEER_HEADS, PEER_KEYS, tq), lambda i, j: (0, 0, i))
    return pl.pallas_call(
        functools.partial(_peer_kernel, n_e1=n_e1, final_norm=final_norm),
        grid=(t // tq, e // te),
        in_specs=[pl.BlockSpec((tq, d), lambda i, j: (i, 0)),
                  sel_spec, sel_spec, sel_spec, sel_spec,
                  pl.BlockSpec((te, d), lambda i, j: (j, 0)),
                  pl.BlockSpec((te, d), lambda i, j: (j, 0)),
                  pl.BlockSpec((tq, d), lambda i, j: (i, 0)),
                  pl.BlockSpec((1, d), lambda i, j: (0, 0)),
                  pl.BlockSpec((1, d), lambda i, j: (0, 0))],
        out_specs=pl.BlockSpec((tq, d), lambda i, j: (i, 0)),
        out_shape=jax.ShapeDtypeStruct((t, d), F32),
        scratch_shapes=[pltpu.VMEM((tq, d), F32)],
        compiler_params=_cparams(("parallel", "arbitrary")),
        name="peer_experts",
    )(h, thr_t, c_t, s2_t, p2_t, u_tab, v_tab, x, g_f.reshape(1, d).astype(F32),
      final_g.reshape(1, d).astype(F32))


_CAND_PAIRS = [(a, b) for a in range(PEER_TOPK) for b in range(PEER_TOPK // (a + 1))]
_CAND_ROWS = -(-len(_CAND_PAIRS) // 8) * 8


def _top_values(work, n):
    vals = []
    for k in range(n):
        m = jnp.max(work, axis=0, keepdims=True)
        vals.append(m)
        if k + 1 < n:
            work = jnp.where(work >= m, -jnp.inf, work)
    return vals


def _peer_select_kernel(q_ref, keys_ref, thr_ref, c_ref, s2_ref, p2_ref, s_scr, cand_scr):
    tq = q_ref.shape[0]
    for p in range(2):
        s_scr[p] = lax.dot_general(keys_ref[p], q_ref[:, p * PEER_HALF:(p + 1) * PEER_HALF],
                                   (((1,), (1,)), ((), ())), preferred_element_type=F32,
                                   precision=lax.Precision.HIGHEST)
    cand_scr[...] = jnp.full(cand_scr.shape, -jnp.inf, F32)
    for ch in range(tq // LANES):
        sl = slice(ch * LANES, (ch + 1) * LANES)
        s1 = s_scr[0, :, sl]
        s2 = s_scr[1, :, sl]
        top1 = _top_values(s1, PEER_TOPK)
        top2 = _top_values(s2, PEER_TOPK)
        for r, (a, b) in enumerate(_CAND_PAIRS):
            cand_scr[pl.ds(r, 1), :] = top1[a] + top2[b]
        best = _top_values(cand_scr[...], PEER_TOPK + 1)
        tau = 0.5 * (best[PEER_TOPK - 1] + best[PEER_TOPK])
        z = 1.0
        for k in range(1, PEER_TOPK):
            z = z + jnp.exp(best[k] - best[0])
        c = jnp.where(s1 >= top1[PEER_TOPK - 1], jnp.exp(s1 - top1[0]), 0.0) * (1.0 / z)
        p2 = jnp.where(s2 >= top2[PEER_TOPK - 1], jnp.exp(s2 - top2[0]), 0.0)
        thr_ref[0, :, sl] = tau - s1
        c_ref[0, :, sl] = c
        s2_ref[0, :, sl] = s2
        p2_ref[0, :, sl] = p2


def _peer_selection(q, sub_keys, *, tq=512):
    t = q.shape[0]
    tq = _tile(t, tq)
    out = jax.ShapeDtypeStruct((PEER_HEADS, PEER_KEYS, t), F32)
    spec = pl.BlockSpec((1, PEER_KEYS, tq), lambda i, h: (h, 0, i))
    return pl.pallas_call(
        _peer_select_kernel,
        grid=(t // tq, PEER_HEADS),
        in_specs=[pl.BlockSpec((tq, PEER_QDIM), lambda i, h: (i, h)),
                  pl.BlockSpec((2, PEER_KEYS, PEER_HALF), lambda i, h: (0, 0, 0))],
        out_specs=[spec, spec, spec, spec],
        out_shape=[out, out, out, out],
        scratch_shapes=[pltpu.VMEM((2, PEER_KEYS, tq), F32),
                        pltpu.VMEM((_CAND_ROWS, LANES), F32)],
        compiler_params=_cparams(("parallel", "parallel")),
        name="peer_select",
    )(q, sub_keys)


def _peer(x, g, shift, scale, gate, w_q, sub_keys, u_tab, v_tab, final_g, final_norm):
    q, h = _norm_mod_matmul(x, g, shift, scale, w_q.astype(BF16), emit_h=True)
    thr_t, c_t, s2_t, p2_t = _peer_selection(q, sub_keys)
    return _peer_experts(h, thr_t, c_t, s2_t, p2_t, u_tab.astype(BF16), v_tab.astype(BF16),
                         x, gate, final_g, final_norm=final_norm)


def _dwconv3(u, w, b=None):
    up = jnp.pad(u, ((1, 1), (0, 0)))
    y = up[:-2] * w[0] + up[1:-1] * w[1] + up[2:] * w[2]
    return y if b is None else y + b


def _rope_tables(n_rows):
    rr, cc = jnp.meshgrid(jnp.arange(n_rows), jnp.arange(GRID_W), indexing='ij')
    inv = ROPE_THETA ** (-jnp.arange(ROPE_AXIS // 2, dtype=F32) / (ROPE_AXIS // 2))
    ar = rr.reshape(-1, 1).astype(F32) * inv
    ac = cc.reshape(-1, 1).astype(F32) * inv
    ang = jnp.concatenate([ar, ar, ac, ac], axis=-1)
    return jnp.cos(ang), jnp.sin(ang)


def _rope(u, cos, sin):
    us = u.reshape(u.shape[:-1] + (2, 2, ROPE_AXIS // 2))
    rot = jnp.stack([-us[..., 1, :], us[..., 0, :]], axis=-2).reshape(u.shape)
    return u * cos[:, None, :] + rot * sin[:, None, :]


def _mixer_conv_mla(x, ctx, g, sh, sc, mc, gate, cos, sin, w_in, conv_w, q_norm_g, w_uq,
                    kv_norm_g, w_ukv, w_out):
    l, d = x.shape
    c0 = 3 * CONV_DIM
    n_in = w_in.shape[1]
    n_pad = -n_in % 512
    w_in_b = jnp.pad(w_in, ((0, 0), (0, n_pad))).astype(BF16)
    p = _norm_mod_matmul(x, g, sh, sc, w_in_b)
    y_conv = p[:, :CONV_DIM] * _dwconv3(p[:, CONV_DIM:2 * CONV_DIM] * p[:, 2 * CONV_DIM:c0], conv_w)
    q_c = p[:, c0:c0 + MLA_Q_RANK]
    kv_c = p[:, c0 + MLA_Q_RANK:c0 + MLA_Q_RANK + MLA_KV_RANK]
    k_r = p[:, c0 + MLA_Q_RANK + MLA_KV_RANK:n_in]
    zq = jnp.zeros((MLA_Q_RANK,), F32)
    zkv = jnp.zeros((MLA_KV_RANK,), F32)
    w_ukv_b = w_ukv.astype(BF16)
    q = _norm_mod_matmul(q_c, q_norm_g, zq, zq, w_uq.astype(BF16)).reshape(l, MLA_HEADS, MLA_QK)
    q = jnp.concatenate([q[..., :MLA_NOPE], _rope(q[..., MLA_NOPE:], cos, sin)], axis=-1)
    q = jnp.transpose(q * ATTN_SCALE, (1, 0, 2)).astype(BF16)
    kv = _norm_mod_matmul(kv_c, kv_norm_g, zkv, zkv, w_ukv_b).reshape(l, MLA_HEADS, MLA_NOPE + MLA_V)
    k_r = _rope(k_r[:, None, :], cos, sin)
    n_c = n_in - (c0 + MLA_Q_RANK)
    w_ctx = jnp.pad(w_in[:, c0 + MLA_Q_RANK:], ((0, 0), (0, -n_c % LANES))).astype(BF16)
    pc = _norm_mod_matmul(ctx, g, mc[:d], mc[d:], w_ctx)
    lc = ctx.shape[0]
    kv_x = _norm_mod_matmul(pc[:, :MLA_KV_RANK], kv_norm_g, zkv, zkv, w_ukv_b).reshape(
        lc, MLA_HEADS, MLA_NOPE + MLA_V)
    k_rx = pc[:, MLA_KV_RANK:n_c][:, None, :]

    def keys(kvm, kr):
        return jnp.concatenate(
            [kvm[..., :MLA_NOPE], jnp.broadcast_to(kr, kvm.shape[:2] + (MLA_ROPE,))], axis=-1)

    k = jnp.concatenate([keys(kv_x, k_rx), keys(kv, k_r)], axis=0)
    v = jnp.concatenate([kv_x[..., MLA_NOPE:], kv[..., MLA_NOPE:]], axis=0)
    k = jnp.transpose(k, (1, 0, 2)).astype(BF16)
    v = jnp.transpose(v, (1, 0, 2)).astype(BF16)
    y_att = _attention(q, k, v)
    y = jnp.concatenate([y_conv.astype(BF16), y_att], axis=-1)
    return _matmul_residual(y, w_out.astype(BF16), x, gate)


def _hyena_filters(l, w1, b1, f1, w2, b2, f2, w3):
    hi = lax.Precision.HIGHEST
    t = jnp.linspace(0.0, 1.0, l, dtype=F32)[:, None]
    bands = jnp.linspace(1e-4, HY_BANDS - 1, HY_BANDS, dtype=F32)
    w = (2.0 * math.pi / l) * jnp.arange(l, dtype=F32)[:, None]
    z = jnp.concatenate([t, jnp.cos(bands * w), -jnp.sin(bands * w)], axis=-1)
    a = jnp.sin(f1 * (jnp.dot(z, w1, precision=hi) + b1))
    a = jnp.sin(f2 * (jnp.dot(a, w2, precision=hi) + b2))
    hf = jnp.dot(a, w3, precision=hi)
    deltas = jnp.abs(jnp.linspace(math.log(HY_TARGET) / HY_SLOW, math.log(HY_TARGET) / HY_FAST,
                                  HY_DIM, dtype=F32))
    decay = jnp.exp(-t * deltas)
    h_fwd = hf[:, :HY_DIM] * decay
    h_bwd = hf[:, HY_DIM:] * decay
    k = jnp.concatenate([h_fwd, jnp.zeros((1, HY_DIM), F32), h_bwd[:0:-1]], axis=0)
    return k / (jnp.sum(jnp.abs(k), axis=0, keepdims=True) + 1e-6)


def _fft_long_conv(u, k):
    l = u.shape[0]
    uf = jnp.fft.rfft(u, n=2 * l, axis=0)
    kf = jnp.fft.rfft(k, n=2 * l, axis=0)
    return jnp.fft.irfft(uf * kf, n=2 * l, axis=0)[:l]


def _mixer_gmlp_hyena(x, g, sh, sc, gate, w_in, gm_norm_g, gm_ws, gm_b, hy_conv_w, hy_conv_b,
                      hy_w1, hy_b1, hy_f1, hy_w2, hy_b2, hy_f2, hy_w3, hy_bias, w_out):
    l, d = x.shape
    p = _norm_mod_matmul(x, g, sh, sc, w_in.astype(BF16))
    z = jax.nn.gelu(p[:, :2 * GM_DIM])
    u, v = z[:, :GM_DIM], z[:, GM_DIM:]
    v = v * lax.rsqrt(jnp.mean(v * v, axis=-1, keepdims=True) + NORM_EPS) * gm_norm_g
    v = v.reshape(l // GM_CHUNK, GM_CHUNK, GM_GROUPS, GM_DIM // GM_GROUPS)
    s = jnp.einsum('gpq,nqgc->npgc', gm_ws, v) + gm_b.T[:, :, None]
    y_gm = u * s.reshape(l, GM_DIM)
    xs = _dwconv3(p[:, 2 * GM_DIM:], hy_conv_w, hy_conv_b)
    x0, x1, hv = xs[:, :HY_DIM], xs[:, HY_DIM:2 * HY_DIM], xs[:, 2 * HY_DIM:]
    filt = _hyena_filters(l, hy_w1, hy_b1, hy_f1, hy_w2, hy_b2, hy_f2, hy_w3)
    g1 = x1 * hv
    y_hy = x0 * (_fft_long_conv(g1, filt) + hy_bias * g1)
    y = jnp.concatenate([y_gm, y_hy], axis=-1).astype(BF16)
    return _matmul_residual(y, w_out.astype(BF16), x, gate)


def kernel(x, c, ctx, c_ctx, ln1_g0, w_mod0, b_mod0, w_in0, conv_w0, q_norm_g0, w_uq0, kv_norm_g0, w_ukv0, w_out0, ln2_g0, peer_wq0, peer_keys0, peer_u0, peer_v0, ln1_g1, w_mod1, b_mod1, w_in1, gm_norm_g1, gm_ws1, gm_b1, hy_conv_w1, hy_conv_b1, hy_w1, hy_b1, hy_f1, hy_w2, hy_b2, hy_f2, hy_w3, hy_bias1, w_out1, ln2_g1, peer_wq1, peer_keys1, peer_u1, peer_v1, final_g):
    b, l, d = x.shape
    assert b == 1 and d == D_MODEL
    xt = x.reshape(l, d)
    cos, sin = _rope_tables(l // GRID_W)
    rows = jnp.zeros((8, d), F32).at[0].set(c[0]).at[1].set(c_ctx)

    mod0 = _mod_vectors(rows, w_mod0, b_mod0)
    sh_m, sc_m, g_m, sh_f, sc_f, g_f = [mod0[0, j * d:(j + 1) * d] for j in range(N_MOD)]
    mc = mod0[1, :2 * d]
    xt = _mixer_conv_mla(xt, ctx[0], ln1_g0, sh_m, sc_m, mc, g_m, cos, sin, w_in0, conv_w0,
                         q_norm_g0, w_uq0, kv_norm_g0, w_ukv0, w_out0)
    xt = _peer(xt, ln2_g0, sh_f, sc_f, g_f, peer_wq0, peer_keys0, peer_u0, peer_v0, final_g,
               final_norm=False)

    mod1 = _mod_vectors(rows, w_mod1, b_mod1)
    sh_m, sc_m, g_m, sh_f, sc_f, g_f = [mod1[0, j * d:(j + 1) * d] for j in range(N_MOD)]
    xt = _mixer_gmlp_hyena(xt, ln1_g1, sh_m, sc_m, g_m, w_in1, gm_norm_g1, gm_ws1, gm_b1,
                           hy_conv_w1, hy_conv_b1, hy_w1, hy_b1, hy_f1, hy_w2, hy_b2, hy_f2,
                           hy_w3, hy_bias1, w_out1)
    xt = _peer(xt, ln2_g1, sh_f, sc_f, g_f, peer_wq1, peer_keys1, peer_u1, peer_v1, final_g,
               final_norm=True)
    return xt.reshape(b, l, d)
```

```python
import functools
import math

import jax
import jax.numpy as jnp
from jax import lax
from jax.experimental import pallas as pl
from jax.experimental.pallas import tpu as pltpu

F32 = jnp.float32
BF16 = jnp.bfloat16

D_MODEL = 2048
GRID_W = 64
N_MOD = 6
NORM_EPS = 1e-6
CONV_DIM = D_MODEL // 2
MLA_HEADS = 8
MLA_NOPE = 128
MLA_ROPE = 64
MLA_V = 128
MLA_QK = MLA_NOPE + MLA_ROPE
MLA_Q_RANK = D_MODEL // 4
MLA_KV_RANK = D_MODEL // 8
ATTN_SCALE = MLA_QK ** -0.5
ROPE_AXIS = MLA_ROPE // 2
ROPE_THETA = 10000.0
GM_DIM = D_MODEL // 2
GM_GROUPS = 8
GM_CHUNK = 128
HY_DIM = D_MODEL // 2
HY_BANDS = 16
HY_TARGET = 1e-2
HY_FAST = 0.3
HY_SLOW = 1.5
PEER_KEYS = 128
PEER_HEADS = 8
PEER_TOPK = 16
PEER_QDIM = 256
PEER_HALF = PEER_QDIM // 2

LANES = 128
VMEM_LIMIT = 56 * 1024 * 1024


def _cparams(sem):
    return pltpu.CompilerParams(dimension_semantics=sem, vmem_limit_bytes=VMEM_LIMIT)


def _tile(n, target, mult=LANES):
    if n <= target:
        return n
    best = None
    for t in range(mult, target + 1, mult):
        if n % t == 0:
            best = t
    assert best is not None, (n, target, mult)
    return best


def _modvec_kernel(a_ref, w_ref, b_ref, o_ref):
    a = a_ref[...]
    a = a * jax.nn.sigmoid(a)
    o_ref[...] = jnp.dot(a, w_ref[...], preferred_element_type=F32,
                         precision=lax.Precision.HIGHEST) + b_ref[...]


def _mod_vectors(rows, w_mod, b_mod):
    d, n = w_mod.shape
    tn = _tile(n, 1024)
    return pl.pallas_call(
        _modvec_kernel,
        grid=(n // tn,),
        in_specs=[pl.BlockSpec((8, d), lambda j: (0, 0)),
                  pl.BlockSpec((d, tn), lambda j: (0, j)),
                  pl.BlockSpec((1, tn), lambda j: (0, j))],
        out_specs=pl.BlockSpec((8, tn), lambda j: (0, j)),
        out_shape=jax.ShapeDtypeStruct((8, n), F32),
        compiler_params=_cparams(("arbitrary",)),
        name="mod_vectors",
    )(rows, w_mod, b_mod.reshape(1, n))


def _nmm_kernel(x_ref, g_ref, sh_ref, sc_ref, w_ref, *rest, emit_h):
    if emit_h:
        o_ref, ho_ref, h_ref = rest
    else:
        o_ref, h_ref = rest

    @pl.when(pl.program_id(1) == 0)
    def _():
        xf = x_ref[...].astype(F32)
        ms = jnp.mean(xf * xf, axis=-1, keepdims=True)
        y = xf * lax.rsqrt(ms + NORM_EPS) * g_ref[...]
        h = (y * (1.0 + sc_ref[...]) + sh_ref[...]).astype(BF16)
        h_ref[...] = h
        if emit_h:
            ho_ref[...] = h

    o_ref[...] = jnp.dot(h_ref[...], w_ref[...], preferred_element_type=F32).astype(o_ref.dtype)


def _norm_mod_matmul(x, g, shift, scale, w, *, out_dtype=F32, tm=512, tn=512, emit_h=False):
    m, k = x.shape
    n = w.shape[1]
    tm = _tile(m, tm, 8)
    tn = _tile(n, tn)
    row = lambda v: v.reshape(1, k).astype(F32)
    out_shape = [jax.ShapeDtypeStruct((m, n), out_dtype)]
    out_specs = [pl.BlockSpec((tm, tn), lambda i, j: (i, j))]
    if emit_h:
        out_shape.append(jax.ShapeDtypeStruct((m, k), BF16))
        out_specs.append(pl.BlockSpec((tm, k), lambda i, j: (i, 0)))
    res = pl.pallas_call(
        functools.partial(_nmm_kernel, emit_h=emit_h),
        grid=(m // tm, n // tn),
        in_specs=[pl.BlockSpec((tm, k), lambda i, j: (i, 0)),
                  pl.BlockSpec((1, k), lambda i, j: (0, 0)),
                  pl.BlockSpec((1, k), lambda i, j: (0, 0)),
                  pl.BlockSpec((1, k), lambda i, j: (0, 0)),
                  pl.BlockSpec((k, tn), lambda i, j: (0, j))],
        out_specs=out_specs,
        out_shape=out_shape,
        scratch_shapes=[pltpu.VMEM((tm, k), BF16)],
        compiler_params=_cparams(("parallel", "arbitrary")),
        name="norm_mod_matmul",
    )(x, row(g), row(shift), row(scale), w)
    return res if emit_h else res[0]


def _mmres_kernel(y_ref, w_ref, x_ref, gate_ref, o_ref):
    o_ref[...] = x_ref[...] + gate_ref[...] * jnp.dot(
        y_ref[...], w_ref[...], preferred_element_type=F32)


def _matmul_residual(y, w, x, gate, *, tm=512, tn=512):
    m, k = y.shape
    n = w.shape[1]
    tm = _tile(m, tm, 8)
    tn = _tile(n, tn)
    return pl.pallas_call(
        _mmres_kernel,
        grid=(m // tm, n // tn),
        in_specs=[pl.BlockSpec((tm, k), lambda i, j: (i, 0)),
                  pl.BlockSpec((k, tn), lambda i, j: (0, j)),
                  pl.BlockSpec((tm, tn), lambda i, j: (i, j)),
                  pl.BlockSpec((1, tn), lambda i, j: (0, j))],
        out_specs=pl.BlockSpec((tm, tn), lambda i, j: (i, j)),
        out_shape=jax.ShapeDtypeStruct((m, n), F32),
        compiler_params=_cparams(("parallel", "parallel")),
        name="matmul_residual",
    )(y, w, x, gate.reshape(1, n).astype(F32))


def _flash_kernel(q_ref, k_ref, v_ref, o_ref, m_ref, l_ref, acc_ref):
    kv = pl.program_id(2)

    @pl.when(kv == 0)
    def _():
        m_ref[...] = jnp.full(m_ref.shape, -jnp.inf, F32)
        l_ref[...] = jnp.zeros(l_ref.shape, F32)
        acc_ref[...] = jnp.zeros(acc_ref.shape, F32)

    s = lax.dot_general(q_ref[0], k_ref[0], (((1,), (1,)), ((), ())),
                        preferred_element_type=F32)
    m_prev = m_ref[...]
    m_new = jnp.maximum(m_prev, jnp.max(s, axis=-1, keepdims=True))
    alpha = jnp.exp(m_prev - m_new)
    p = jnp.exp(s - m_new)
    l_ref[...] = alpha * l_ref[...] + jnp.sum(p, axis=-1, keepdims=True)
    acc_ref[...] = alpha * acc_ref[...] + jnp.dot(
        p.astype(BF16), v_ref[0], preferred_element_type=F32)
    m_ref[...] = m_new

    @pl.when(kv == pl.num_programs(2) - 1)
    def _():
        o_ref[...] = (acc_ref[...] / l_ref[...]).astype(o_ref.dtype)


def _attention(q, k, v, *, tq=512, tk=2048):
    h, l, dqk = q.shape
    lk = k.shape[1]
    dv = v.shape[2]
    tq = _tile(l, tq)
    tk = _tile(lk, tk)
    return pl.pallas_call(
        _flash_kernel,
        grid=(h, l // tq, lk // tk),
        in_specs=[pl.BlockSpec((1, tq, dqk), lambda hh, i, j: (hh, i, 0)),
                  pl.BlockSpec((1, tk, dqk), lambda hh, i, j: (hh, j, 0)),
                  pl.BlockSpec((1, tk, dv), lambda hh, i, j: (hh, j, 0))],
        out_specs=pl.BlockSpec((tq, dv), lambda hh, i, j: (i, hh)),
        out_shape=jax.ShapeDtypeStruct((l, h * dv), BF16),
        scratch_shapes=[pltpu.VMEM((tq, 1), F32), pltpu.VMEM((tq, 1), F32),
                        pltpu.VMEM((tq, dv), F32)],
        compiler_params=_cparams(("parallel", "parallel", "arbitrary")),
        name="mla_attention",
    )(q, k, v)


def _peer_kernel(h_ref, thr_ref, c_ref, s2_ref, p2_ref, u_ref, v_ref, x_ref, gf_ref, fg_ref,
                 o_ref, acc_ref, *, n_e1, final_norm):
    j = pl.program_id(1)

    @pl.when(j == 0)
    def _():
        acc_ref[...] = jnp.zeros(acc_ref.shape, F32)

    a_t = lax.dot_general(u_ref[...], h_ref[...], (((1,), (1,)), ((), ())),
                          preferred_element_type=F32)
    blocks = []
    for r in range(n_e1):
        e1 = j * n_e1 + r
        w = None
        for hd in range(PEER_HEADS):
            thr_row = thr_ref[hd, pl.ds(e1, 1), :]
            c_row = c_ref[hd, pl.ds(e1, 1), :]
            term = jnp.where(s2_ref[hd] >= thr_row, p2_ref[hd], 0.0) * c_row
            w = term if w is None else w + term
        blocks.append(w)
    w_t = blocks[0] if n_e1 == 1 else jnp.concatenate(blocks, axis=0)
    wa_t = w_t * jax.nn.gelu(a_t)
    wa = wa_t.T.astype(BF16)
    acc_ref[...] += jnp.dot(wa, v_ref[...], preferred_element_type=F32)

    @pl.when(j == pl.num_programs(1) - 1)
    def _():
        y = x_ref[...] + gf_ref[...] * acc_ref[...]
        if final_norm:
            ms = jnp.mean(y * y, axis=-1, keepdims=True)
            y = y * lax.rsqrt(ms + NORM_EPS) * fg_ref[...]
        o_ref[...] = y


def _peer_experts(h, thr_t, c_t, s2_t, p2_t, u_tab, v_tab, x, g_f, final_g, *, final_norm,
                  tq=512, te=512):
    t, d = h.shape
    e = u_tab.shape[0]
    tq = _tile(t, tq)
    n_e1 = te // PEER_KEYS
    sel_spec = pl.BlockSpec((PEER_HEADS, PEER_KEYS, tq), lambda i, j: (0, 0, i))
    return pl.pallas_call(
        functools.partial(_peer_kernel, n_e1=n_e1, final_norm=final_norm),
        grid=(t // tq, e // te),
        in_specs=[pl.BlockSpec((tq, d), lambda i, j: (i, 0)),
                  sel_spec, sel_spec, sel_spec, sel_spec,
                  pl.BlockSpec((te, d), lambda i, j: (j, 0)),
                  pl.BlockSpec((te, d), lambda i, j: (j, 0)),
                  pl.BlockSpec((tq, d), lambda i, j: (i, 0)),
                  pl.BlockSpec((1, d), lambda i, j: (0, 0)),
                  pl.BlockSpec((1, d), lambda i, j: (0, 0))],
        out_specs=pl.BlockSpec((tq, d), lambda i, j: (i, 0)),
        out_shape=jax.ShapeDtypeStruct((t, d), F32),
        scratch_shapes=[pltpu.VMEM((tq, d), F32)],
        compiler_params=_cparams(("parallel", "arbitrary")),
        name="peer_experts",
    )(h, thr_t, c_t, s2_t, p2_t, u_tab, v_tab, x, g_f.reshape(1, d).astype(F32),
      final_g.reshape(1, d).astype(F32))


_CAND_PAIRS = [(a, b) for a in range(PEER_TOPK) for b in range(PEER_TOPK // (a + 1))]
_CAND_ROWS = -(-len(_CAND_PAIRS) // 8) * 8


def _top_values(work, n):
    vals = []
    for k in range(n):
        m = jnp.max(work, axis=0, keepdims=True)
        vals.append(m)
        if k + 1 < n:
            work = jnp.where(work >= m, -jnp.inf, work)
    return vals


def _peer_select_kernel(q_ref, keys_ref, thr_ref, c_ref, s2_ref, p2_ref, s_scr, cand_scr):
    tq = q_ref.shape[0]
    for p in range(2):
        s_scr[p] = lax.dot_general(keys_ref[p], q_ref[:, p * PEER_HALF:(p + 1) * PEER_HALF],
                                   (((1,), (1,)), ((), ())), preferred_element_type=F32,
                                   precision=lax.Precision.HIGHEST)
    cand_scr[...] = jnp.full(cand_scr.shape, -jnp.inf, F32)
    for ch in range(tq // LANES):
        sl = slice(ch * LANES, (ch + 1) * LANES)
        s1 = s_scr[0, :, sl]
        s2 = s_scr[1, :, sl]
        top1 = _top_values(s1, PEER_TOPK)
        top2 = _top_values(s2, PEER_TOPK)
        for r, (a, b) in enumerate(_CAND_PAIRS):
            cand_scr[pl.ds(r, 1), :] = top1[a] + top2[b]
        best = _top_values(cand_scr[...], PEER_TOPK + 1)
        tau = 0.5 * (best[PEER_TOPK - 1] + best[PEER_TOPK])
        z = 1.0
        for k in range(1, PEER_TOPK):
            z = z + jnp.exp(best[k] - best[0])
        c = jnp.where(s1 >= top1[PEER_TOPK - 1], jnp.exp(s1 - top1[0]), 0.0) * (1.0 / z)
        p2 = jnp.where(s2 >= top2[PEER_TOPK - 1], jnp.exp(s2 - top2[0]), 0.0)
        thr_ref[0, :, sl] = tau - s1
        c_ref[0, :, sl] = c
        s2_ref[0, :, sl] = s2
        p2_ref[0, :, sl] = p2


def _peer_selection(q, sub_keys, *, tq=512):
    t = q.shape[0]
    tq = _tile(t, tq)
    out = jax.ShapeDtypeStruct((PEER_HEADS, PEER_KEYS, t), F32)
    spec = pl.BlockSpec((1, PEER_KEYS, tq), lambda i, h: (h, 0, i))
    return pl.pallas_call(
        _peer_select_kernel,
        grid=(t // tq, PEER_HEADS),
        in_specs=[pl.BlockSpec((tq, PEER_QDIM), lambda i, h: (i, h)),
                  pl.BlockSpec((2, PEER_KEYS, PEER_HALF), lambda i, h: (0, 0, 0))],
        out_specs=[spec, spec, spec, spec],
        out_shape=[out, out, out, out],
        scratch_shapes=[pltpu.VMEM((2, PEER_KEYS, tq), F32),
                        pltpu.VMEM((_CAND_ROWS, LANES), F32)],
        compiler_params=_cparams(("parallel", "parallel")),
        name="peer_select",
    )(q, sub_keys)


def _peer(x, g, shift, scale, gate, w_q, sub_keys, u_tab, v_tab, final_g, final_norm):
    q, h = _norm_mod_matmul(x, g, shift, scale, w_q.astype(BF16), emit_h=True)
    thr_t, c_t, s2_t, p2_t = _peer_selection(q, sub_keys)
    return _peer_experts(h, thr_t, c_t, s2_t, p2_t, u_tab.astype(BF16), v_tab.astype(BF16),
                         x, gate, final_g, final_norm=final_norm)


def _dwconv3(u, w, b=None):
    up = jnp.pad(u, ((1, 1), (0, 0)))
    y = up[:-2] * w[0] + up[1:-1] * w[1] + up[2:] * w[2]
    return y if b is None else y + b


def _rope_tables(n_rows):
    rr, cc = jnp.meshgrid(jnp.arange(n_rows), jnp.arange(GRID_W), indexing='ij')
    inv = ROPE_THETA ** (-jnp.arange(ROPE_AXIS // 2, dtype=F32) / (ROPE_AXIS // 2))
    ar = rr.reshape(-1, 1).astype(F32) * inv
    ac = cc.reshape(-1, 1).astype(F32) * inv
    ang = jnp.concatenate([ar, ar, ac, ac], axis=-1)
    return jnp.cos(ang), jnp.sin(ang)


def _rope(u, cos, sin):
    us = u.reshape(u.shape[:-1] + (2, 2, ROPE_AXIS // 2))
    rot = jnp.stack([-us[..., 1, :], us[..., 0, :]], axis=-2).reshape(u.shape)
    return u * cos[:, None, :] + rot * sin[:, None, :]


def _mixer_conv_mla(x, ctx, g, sh, sc, mc, gate, cos, sin, w_in, conv_w, q_norm_g, w_uq,
                    kv_norm_g, w_ukv, w_out):
    l, d = x.shape
    c0 = 3 * CONV_DIM
    n_in = w_in.shape[1]
    n_pad = -n_in % 512
    w_in_b = jnp.pad(w_in, ((0, 0), (0, n_pad))).astype(BF16)
    p = _norm_mod_matmul(x, g, sh, sc, w_in_b)
    y_conv = p[:, :CONV_DIM] * _dwconv3(p[:, CONV_DIM:2 * CONV_DIM] * p[:, 2 * CONV_DIM:c0], conv_w)
    q_c = p[:, c0:c0 + MLA_Q_RANK]
    kv_c = p[:, c0 + MLA_Q_RANK:c0 + MLA_Q_RANK + MLA_KV_RANK]
    k_r = p[:, c0 + MLA_Q_RANK + MLA_KV_RANK:n_in]
    zq = jnp.zeros((MLA_Q_RANK,), F32)
    zkv = jnp.zeros((MLA_KV_RANK,), F32)
    w_ukv_b = w_ukv.astype(BF16)
    q = _norm_mod_matmul(q_c, q_norm_g, zq, zq, w_uq.astype(BF16)).reshape(l, MLA_HEADS, MLA_QK)
    q = jnp.concatenate([q[..., :MLA_NOPE], _rope(q[..., MLA_NOPE:], cos, sin)], axis=-1)
    q = jnp.transpose(q * ATTN_SCALE, (1, 0, 2)).astype(BF16)
    kv = _norm_mod_matmul(kv_c, kv_norm_g, zkv, zkv, w_ukv_b).reshape(l, MLA_HEADS, MLA_NOPE + MLA_V)
    k_r = _rope(k_r[:, None, :], cos, sin)
    n_c = n_in - (c0 + MLA_Q_RANK)
    w_ctx = jnp.pad(w_in[:, c0 + MLA_Q_RANK:], ((0, 0), (0, -n_c % LANES))).astype(BF16)
    pc = _norm_mod_matmul(ctx, g, mc[:d], mc[d:], w_ctx)
    lc = ctx.shape[0]
    kv_x = _norm_mod_matmul(pc[:, :MLA_KV_RANK], kv_norm_g, zkv, zkv, w_ukv_b).reshape(
        lc, MLA_HEADS, MLA_NOPE + MLA_V)
    k_rx = pc[:, MLA_KV_RANK:n_c][:, None, :]

    def keys(kvm, kr):
        return jnp.concatenate(
            [kvm[..., :MLA_NOPE], jnp.broadcast_to(kr, kvm.shape[:2] + (MLA_ROPE,))], axis=-1)

    k = jnp.concatenate([keys(kv_x, k_rx), keys(kv, k_r)], axis=0)
    v = jnp.concatenate([kv_x[..., MLA_NOPE:], kv[..., MLA_NOPE:]], axis=0)
    k = jnp.transpose(k, (1, 0, 2)).astype(BF16)
    v = jnp.transpose(v, (1, 0, 2)).astype(BF16)
    y_att = _attention(q, k, v)
    y = jnp.concatenate([y_conv.astype(BF16), y_att], axis=-1)
    return _matmul_residual(y, w_out.astype(BF16), x, gate)


def _gmlp_kernel(u_ref, v_ref, g_ref, ws_ref, b_ref, o_ref, vn_ref):
    v = jax.nn.gelu(v_ref[...])
    ms = jnp.mean(v * v, axis=-1, keepdims=True)
    vn_ref[...] = (v * lax.rsqrt(ms + NORM_EPS) * g_ref[...]).astype(BF16)
    gw = GM_DIM // GM_GROUPS
    for ci in range(u_ref.shape[0] // GM_CHUNK):
        rows = slice(ci * GM_CHUNK, (ci + 1) * GM_CHUNK)
        for gi in range(GM_GROUPS):
            cols = slice(gi * gw, (gi + 1) * gw)
            s = jnp.dot(ws_ref[gi], vn_ref[rows, cols], preferred_element_type=F32) + b_ref[gi]
            o_ref[rows, cols] = (jax.nn.gelu(u_ref[rows, cols]) * s).astype(o_ref.dtype)


def _gmlp(p, gm_norm_g, gm_ws, gm_b, *, tm=512):
    l = p.shape[0]
    tm = _tile(l, tm, GM_CHUNK)
    return pl.pallas_call(
        _gmlp_kernel,
        grid=(l // tm,),
        in_specs=[pl.BlockSpec((tm, GM_DIM), lambda i: (i, 0)),
                  pl.BlockSpec((tm, GM_DIM), lambda i: (i, 1)),
                  pl.BlockSpec((1, GM_DIM), lambda i: (0, 0)),
                  pl.BlockSpec((GM_GROUPS, GM_CHUNK, GM_CHUNK), lambda i: (0, 0, 0)),
                  pl.BlockSpec((GM_GROUPS, GM_CHUNK, 1), lambda i: (0, 0, 0))],
        out_specs=pl.BlockSpec((tm, GM_DIM), lambda i: (i, 0)),
        out_shape=jax.ShapeDtypeStruct((l, GM_DIM), BF16),
        scratch_shapes=[pltpu.VMEM((tm, GM_DIM), BF16)],
        compiler_params=_cparams(("parallel",)),
        name="gmlp",
    )(p, p, gm_norm_g.reshape(1, GM_DIM), gm_ws.astype(BF16), gm_b.reshape(GM_GROUPS, GM_CHUNK, 1))


def _shift_rows(u, prev_row, next_row):
    n = u.shape[0]
    r = lax.broadcasted_iota(jnp.int32, u.shape, 0)
    up = jnp.where(r == 0, prev_row, pltpu.roll(u, 1, 0))
    dn = jnp.where(r == n - 1, next_row, pltpu.roll(u, n - 1, 0))
    return up, dn


def _hyena_pre_kernel(*refs):
    ins, (w_ref, b_ref, x0_ref, g1_ref) = refs[:9], refs[9:]
    i = pl.program_id(0)
    first = i == 0
    last = i == pl.num_programs(0) - 1
    tc = x0_ref.shape[1]
    outs = []
    for part in range(3):
        main, prev, nxt = ins[3 * part:3 * part + 3]
        u = main[...]
        prev_row = jnp.where(first, 0.0, prev[7:8, :])
        next_row = jnp.where(last, 0.0, nxt[0:1, :])
        up, dn = _shift_rows(u, prev_row, next_row)
        w = w_ref[part]
        outs.append(up * w[0:1, :] + u * w[1:2, :] + dn * w[2:3, :] + b_ref[part])
    x0_ref[...] = outs[0]
    g1_ref[...] = outs[1] * outs[2]


def _hyena_pre(p, col0, conv_w, conv_b, *, tm=512, tc=512):
    l = p.shape[0]
    tm = _tile(l, tm, 8)
    nrb = l // 8
    specs = []
    for part in range(3):
        cb = (col0 + part * HY_DIM) // tc
        specs += [
            pl.BlockSpec((tm, tc), lambda i, j, cb=cb: (i, cb + j)),
            pl.BlockSpec((8, tc), lambda i, j, cb=cb: (jnp.maximum(i * (tm // 8) - 1, 0), cb + j)),
            pl.BlockSpec((8, tc), lambda i, j, cb=cb: (jnp.minimum((i + 1) * (tm // 8), nrb - 1), cb + j)),
        ]
    w = jnp.transpose(conv_w.reshape(3, 3, HY_DIM), (1, 0, 2))
    b = conv_b.reshape(3, 1, HY_DIM)
    out = jax.ShapeDtypeStruct((l, HY_DIM), F32)
    ospec = pl.BlockSpec((tm, tc), lambda i, j: (i, j))
    return pl.pallas_call(
        _hyena_pre_kernel,
        grid=(l // tm, HY_DIM // tc),
        in_specs=specs + [pl.BlockSpec((3, 3, tc), lambda i, j: (0, 0, j)),
                          pl.BlockSpec((3, 1, tc), lambda i, j: (0, 0, j))],
        out_specs=[ospec, ospec],
        out_shape=[out, out],
        compiler_params=_cparams(("parallel", "parallel")),
        name="hyena_pre",
    )(*([p] * 9), w, b)


def _hyena_filter_kernel(band_ref, phase_ref, w1_ref, b1_ref, f1_ref, w2_ref, b2_ref, f2_ref,
                         w3_ref, delta_ref, k_ref, sum_ref, *, seq):
    i = pl.program_id(0)
    tm = k_ref.shape[0]
    hi = lax.Precision.HIGHEST
    r = i * tm + lax.broadcasted_iota(jnp.int32, (tm, 1), 0)
    lag = jnp.where(r < seq, r, 2 * seq - r).astype(F32)
    t = lag * (1.0 / (seq - 1))
    w = lag * (2.0 * math.pi / seq)
    lane = lax.broadcasted_iota(jnp.int32, (tm, LANES), 1)
    z = jnp.where(lane == 0, t, jnp.where(lane <= 2 * HY_BANDS,
                                          jnp.cos(band_ref[...] * w + phase_ref[...]), 0.0))
    a = jnp.sin(f1_ref[...] * (jnp.dot(z, w1_ref[...], precision=hi,
                                       preferred_element_type=F32) + b1_ref[...]))
    a = jnp.sin(f2_ref[...] * (jnp.dot(a, w2_ref[...], precision=hi,
                                       preferred_element_type=F32) + b2_ref[...]))
    hf = jnp.dot(a, w3_ref[...], precision=hi, preferred_element_type=F32)
    k = jnp.where(r == seq, 0.0, hf * jnp.exp(-t * delta_ref[...]))
    k_ref[...] = k

    @pl.when(i == 0)
    def _():
        sum_ref[...] = jnp.zeros(sum_ref.shape, F32)

    sum_ref[...] += jnp.sum(jnp.abs(k), axis=0, keepdims=True)


def _hyena_filter(seq, w1, b1, f1, w2, b2, f2, w3, *, tm=512):
    import numpy as np
    n = 2 * seq
    tm = _tile(seq, tm, 8)
    bands = np.linspace(1e-4, HY_BANDS - 1, HY_BANDS)
    band = np.zeros((1, LANES), np.float32)
    phase = np.zeros((1, LANES), np.float32)
    band[0, 1:1 + HY_BANDS] = bands
    band[0, 1 + HY_BANDS:1 + 2 * HY_BANDS] = bands
    phase[0, 1 + HY_BANDS:1 + 2 * HY_BANDS] = 0.5 * math.pi
    deltas = np.abs(np.linspace(math.log(HY_TARGET) / HY_SLOW, math.log(HY_TARGET) / HY_FAST,
                                HY_DIM)).astype(np.float32).reshape(1, HY_DIM)
    emb, ffn = w1.shape
    pad2 = lambda a, rr, cc: jnp.pad(a, ((0, rr - a.shape[0]), (0, cc - a.shape[1])))
    row = lambda v: jnp.pad(v.reshape(1, -1), ((0, 0), (0, LANES - v.shape[0])))
    nhalf = seq // tm
    const = lambda shape: pl.BlockSpec(shape, lambda i: (0,) * len(shape))
    return pl.pallas_call(
        functools.partial(_hyena_filter_kernel, seq=seq),
        grid=(n // tm,),
        in_specs=[const((1, LANES)), const((1, LANES)),
                  const((LANES, LANES)), const((1, LANES)), const((1, LANES)),
                  const((LANES, LANES)), const((1, LANES)), const((1, LANES)),
                  pl.BlockSpec((LANES, HY_DIM), lambda i: (0, i // nhalf)),
                  const((1, HY_DIM))],
        out_specs=[pl.BlockSpec((tm, HY_DIM), lambda i: (i, 0)),
                   pl.BlockSpec((1, HY_DIM), lambda i: (0, 0))],
        out_shape=[jax.ShapeDtypeStruct((n, HY_DIM), F32),
                   jax.ShapeDtypeStruct((1, HY_DIM), F32)],
        compiler_params=_cparams(("arbitrary",)),
        name="hyena_filter",
    )(jnp.asarray(band), jnp.asarray(phase), pad2(w1, LANES, LANES), row(b1), row(f1),
      pad2(w2, LANES, LANES), row(b2), row(f2), pad2(w3, LANES, 2 * HY_DIM), jnp.asarray(deltas))


DFT_N2 = 256


def _dft_tables(n1):
    import numpy as np
    n2 = DFT_N2
    n = n1 * n2
    a1 = 2.0 * np.pi * np.outer(np.arange(n1), np.arange(n1)) / n1
    fwd1 = np.concatenate([np.cos(a1), -np.sin(a1)], axis=0)
    inv1 = np.concatenate([np.cos(a1), -np.sin(a1)], axis=1)[:n1 // 2]
    a2 = 2.0 * np.pi * np.outer(np.arange(n2), np.arange(n2)) / n2
    c2, s2 = np.cos(a2), np.sin(a2)
    fwd2 = np.block([[c2, s2], [-s2, c2]])
    inv2 = np.block([[c2, -s2], [s2, c2]])
    th = 2.0 * np.pi * np.outer(np.arange(n2), np.arange(n1)) / n
    bf = lambda a: jnp.asarray(a, dtype=BF16)
    f3 = lambda a: jnp.asarray(a[..., None], dtype=F32)
    return dict(fwd1=bf(fwd1), inv1=bf(inv1), fwd2=bf(fwd2), inv2=bf(inv2),
                twa_c=f3(np.cos(th)), twa_s=f3(np.sin(th)),
                twc_c=f3(np.cos(th).T / n), twc_s=f3(np.sin(th).T / n))


def _dft_a_kernel(x_ref, f_ref, c_ref, s_ref, re_ref, im_ref, *, n1, width):
    for sg in range(x_ref.shape[1] // width):
        cols = slice(sg * width, (sg + 1) * width)
        r = jnp.dot(f_ref[...], x_ref[:, cols].astype(BF16), preferred_element_type=F32)
        re, im = r[:n1], r[n1:]
        c, s = c_ref[sg], s_ref[sg]
        re_ref[:, cols] = (re * c + im * s).astype(re_ref.dtype)
        im_ref[:, cols] = (im * c - re * s).astype(im_ref.dtype)


def _dft_stage_a(x2, tab, n1, width, *, seg=4):
    k, m = x2.shape
    seg = min(seg, DFT_N2)
    tn = seg * width
    out = jax.ShapeDtypeStruct((n1, m), BF16)
    ospec = pl.BlockSpec((n1, tn), lambda j: (0, j))
    tspec = pl.BlockSpec((seg, n1, 1), lambda j: (j, 0, 0))
    return pl.pallas_call(
        functools.partial(_dft_a_kernel, n1=n1, width=width),
        grid=(m // tn,),
        in_specs=[pl.BlockSpec((k, tn), lambda j: (0, j)),
                  pl.BlockSpec((2 * n1, k), lambda j: (0, 0)), tspec, tspec],
        out_specs=[ospec, ospec],
        out_shape=[out, out],
        compiler_params=_cparams(("parallel",)),
        name="dft_stage_a",
    )(x2, tab["fwd1"][:, :k], tab["twa_c"], tab["twa_s"])


def _dft_c_filter_kernel(re_ref, im_ref, f_ref, inv_ref, kre_ref, kim_ref):
    n2 = DFT_N2
    g = (jnp.dot(f_ref[:, :n2], re_ref[0], preferred_element_type=F32)
         + jnp.dot(f_ref[:, n2:], im_ref[0], preferred_element_type=F32))
    kre_ref[0] = g[:n2] * inv_ref[...]
    kim_ref[0] = g[n2:] * inv_ref[...]


def _dft_c_conv_kernel(re_ref, im_ref, kre_ref, kim_ref, f_ref, fi_ref, c_ref, s_ref,
                       ore_ref, oim_ref):
    n2 = DFT_N2
    g = (jnp.dot(f_ref[:, :n2], re_ref[0], preferred_element_type=F32)
         + jnp.dot(f_ref[:, n2:], im_ref[0], preferred_element_type=F32))
    gre, gim = g[:n2], g[n2:]
    kre, kim = kre_ref[0], kim_ref[0]
    yre = (gre * kre - gim * kim).astype(BF16)
    yim = (gre * kim + gim * kre).astype(BF16)
    b = (jnp.dot(fi_ref[:, :n2], yre, preferred_element_type=F32)
         + jnp.dot(fi_ref[:, n2:], yim, preferred_element_type=F32))
    bre, bim = b[:n2], b[n2:]
    c, s = c_ref[0], s_ref[0]
    ore_ref[0] = (bre * c - bim * s).astype(ore_ref.dtype)
    oim_ref[0] = (bre * s + bim * c).astype(oim_ref.dtype)


def _dft_stage_c_filter(re3, im3, tab, inv_norm, *, tc=512):
    n1, n2, c = re3.shape
    tc = _tile(c, tc)
    blk = pl.BlockSpec((1, n2, tc), lambda i, j: (i, 0, j))
    out = jax.ShapeDtypeStruct((n1, n2, c), F32)
    return pl.pallas_call(
        _dft_c_filter_kernel,
        grid=(n1, c // tc),
        in_specs=[blk, blk, pl.BlockSpec((2 * n2, 2 * n2), lambda i, j: (0, 0)),
                  pl.BlockSpec((1, tc), lambda i, j: (0, j))],
        out_specs=[blk, blk],
        out_shape=[out, out],
        compiler_params=_cparams(("parallel", "parallel")),
        name="dft_stage_c_filter",
    )(re3, im3, tab["fwd2"], inv_norm)


def _dft_stage_c_conv(re3, im3, kre, kim, tab, *, tc=512):
    n1, n2, c = re3.shape
    tc = _tile(c, tc)
    blk = pl.BlockSpec((1, n2, tc), lambda i, j: (i, 0, j))
    mat = pl.BlockSpec((2 * n2, 2 * n2), lambda i, j: (0, 0))
    tw = pl.BlockSpec((1, n2, 1), lambda i, j: (i, 0, 0))
    out = jax.ShapeDtypeStruct((n1, n2, c), BF16)
    return pl.pallas_call(
        _dft_c_conv_kernel,
        grid=(n1, c // tc),
        in_specs=[blk, blk, blk, blk, mat, mat, tw, tw],
        out_specs=[blk, blk],
        out_shape=[out, out],
        compiler_params=_cparams(("parallel", "parallel")),
        name="dft_stage_c_conv",
    )(re3, im3, kre, kim, tab["fwd2"], tab["inv2"], tab["twc_c"], tab["twc_s"])


def _dft_a_inv_kernel(re_ref, im_ref, f_ref, x0_ref, g1_ref, bias_ref, o_ref, *, n1):
    y = (jnp.dot(f_ref[:, :n1], re_ref[...], preferred_element_type=F32)
         + jnp.dot(f_ref[:, n1:], im_ref[...], preferred_element_type=F32))
    o_ref[...] = (x0_ref[...] * (y + bias_ref[...] * g1_ref[...])).astype(o_ref.dtype)


def _dft_stage_a_inv(re2, im2, tab, x0_2, g1_2, bias_row, *, tn=4096):
    n1, m = re2.shape
    tn = _tile(m, tn)
    half = n1 // 2
    blk = pl.BlockSpec((n1, tn), lambda j: (0, j))
    hblk = pl.BlockSpec((half, tn), lambda j: (0, j))
    return pl.pallas_call(
        functools.partial(_dft_a_inv_kernel, n1=n1),
        grid=(m // tn,),
        in_specs=[blk, blk, pl.BlockSpec((half, 2 * n1), lambda j: (0, 0)), hblk, hblk,
                  pl.BlockSpec((1, tn), lambda j: (0, 0))],
        out_specs=hblk,
        out_shape=jax.ShapeDtypeStruct((half, m), BF16),
        compiler_params=_cparams(("parallel",)),
        name="dft_stage_a_inv",
    )(re2, im2, tab["inv1"], x0_2, g1_2, bias_row)


def _hyena(p, col0, conv_w, conv_b, w1, b1, f1, w2, b2, f2, w3, bias):
    l = p.shape[0]
    c = HY_DIM
    n1 = 2 * l // DFT_N2
    tab = _dft_tables(n1)
    x0, g1 = _hyena_pre(p, col0, conv_w, conv_b)
    filt, abs_sum = _hyena_filter(l, w1, b1, f1, w2, b2, f2, w3)
    inv_norm = 1.0 / (abs_sum + 1e-6)
    fre, fim = _dft_stage_a(filt.reshape(n1, DFT_N2 * c), tab, n1, c)
    kre, kim = _dft_stage_c_filter(fre.reshape(n1, DFT_N2, c), fim.reshape(n1, DFT_N2, c), tab,
                                   inv_norm)
    are, aim = _dft_stage_a(g1.reshape(n1 // 2, DFT_N2 * c), tab, n1, c)
    bre, bim = _dft_stage_c_conv(are.reshape(n1, DFT_N2, c), aim.reshape(n1, DFT_N2, c),
                                 kre, kim, tab)
    tn = _tile(DFT_N2 * c, 4096)
    bias_row = jnp.tile(bias.reshape(1, c), (1, tn // c))
    y = _dft_stage_a_inv(bre.reshape(n1, DFT_N2 * c), bim.reshape(n1, DFT_N2 * c), tab,
                         x0.reshape(n1 // 2, DFT_N2 * c), g1.reshape(n1 // 2, DFT_N2 * c),
                         bias_row, tn=tn)
    return y.reshape(l, c)


def _mm2res_kernel(y1_ref, y2_ref, w1_ref, w2_ref, x_ref, gate_ref, o_ref):
    acc = jnp.dot(y1_ref[...], w1_ref[...], preferred_element_type=F32)
    acc += jnp.dot(y2_ref[...], w2_ref[...], preferred_element_type=F32)
    o_ref[...] = x_ref[...] + gate_ref[...] * acc


def _matmul2_residual(y1, y2, w, x, gate, *, tm=512, tn=512):
    m, k1 = y1.shape
    k2 = y2.shape[1]
    n = w.shape[1]
    tm = _tile(m, tm, 8)
    tn = _tile(n, tn)
    kb = k1 // k2
    assert k1 == kb * k2
    return pl.pallas_call(
        _mm2res_kernel,
        grid=(m // tm, n // tn),
        in_specs=[pl.BlockSpec((tm, k1), lambda i, j: (i, 0)),
                  pl.BlockSpec((tm, k2), lambda i, j: (i, 0)),
                  pl.BlockSpec((k1, tn), lambda i, j: (0, j)),
                  pl.BlockSpec((k2, tn), lambda i, j: (kb, j)),
                  pl.BlockSpec((tm, tn), lambda i, j: (i, j)),
                  pl.BlockSpec((1, tn), lambda i, j: (0, j))],
        out_specs=pl.BlockSpec((tm, tn), lambda i, j: (i, j)),
        out_shape=jax.ShapeDtypeStruct((m, n), F32),
        compiler_params=_cparams(("parallel", "parallel")),
        name="matmul2_residual",
    )(y1, y2, w, w, x, gate.reshape(1, n).astype(F32))


def _mixer_gmlp_hyena(x, g, sh, sc, gate, w_in, gm_norm_g, gm_ws, gm_b, hy_conv_w, hy_conv_b,
                      hy_w1, hy_b1, hy_f1, hy_w2, hy_b2, hy_f2, hy_w3, hy_bias, w_out):
    p = _norm_mod_matmul(x, g, sh, sc, w_in.astype(BF16))
    y_gm = _gmlp(p, gm_norm_g, gm_ws, gm_b)
    y_hy = _hyena(p, 2 * GM_DIM, hy_conv_w, hy_conv_b, hy_w1, hy_b1, hy_f1, hy_w2, hy_b2,
                  hy_f2, hy_w3, hy_bias)
    return _matmul2_residual(y_gm, y_hy, w_out.astype(BF16), x, gate)


def kernel(x, c, ctx, c_ctx, ln1_g0, w_mod0, b_mod0, w_in0, conv_w0, q_norm_g0, w_uq0, kv_norm_g0, w_ukv0, w_out0, ln2_g0, peer_wq0, peer_keys0, peer_u0, peer_v0, ln1_g1, w_mod1, b_mod1, w_in1, gm_norm_g1, gm_ws1, gm_b1, hy_conv_w1, hy_conv_b1, hy_w1, hy_b1, hy_f1, hy_w2, hy_b2, hy_f2, hy_w3, hy_bias1, w_out1, ln2_g1, peer_wq1, peer_keys1, peer_u1, peer_v1, final_g):
    b, l, d = x.shape
    assert b == 1 and d == D_MODEL
    xt = x.reshape(l, d)
    cos, sin = _rope_tables(l // GRID_W)
    rows = jnp.zeros((8, d), F32).at[0].set(c[0]).at[1].set(c_ctx)

    mod0 = _mod_vectors(rows, w_mod0, b_mod0)
    sh_m, sc_m, g_m, sh_f, sc_f, g_f = [mod0[0, j * d:(j + 1) * d] for j in range(N_MOD)]
    mc = mod0[1, :2 * d]
    xt = _mixer_conv_mla(xt, ctx[0], ln1_g0, sh_m, sc_m, mc, g_m, cos, sin, w_in0, conv_w0,
                         q_norm_g0, w_uq0, kv_norm_g0, w_ukv0, w_out0)
    xt = _peer(xt, ln2_g0, sh_f, sc_f, g_f, peer_wq0, peer_keys0, peer_u0, peer_v0, final_g,
               final_norm=False)

    mod1 = _mod_vectors(rows, w_mod1, b_mod1)
    sh_m, sc_m, g_m, sh_f, sc_f, g_f = [mod1[0, j * d:(j + 1) * d] for j in range(N_MOD)]
    xt = _mixer_gmlp_hyena(xt, ln1_g1, sh_m, sc_m, g_m, w_in1, gm_norm_g1, gm_ws1, gm_b1,
                           hy_conv_w1, hy_conv_b1, hy_w1, hy_b1, hy_f1, hy_w2, hy_b2, hy_f2,
                           hy_w3, hy_bias1, w_out1)
    xt = _peer(xt, ln2_g1, sh_f, sc_f, g_f, peer_wq1, peer_keys1, peer_u1, peer_v1, final_g,
               final_norm=True)
    return xt.reshape(b, l, d)
```

```python
import functools
import math

import jax
import jax.numpy as jnp
from jax import lax
from jax.experimental import pallas as pl
from jax.experimental.pallas import tpu as pltpu

F32 = jnp.float32
BF16 = jnp.bfloat16

D_MODEL = 2048
GRID_W = 64
N_MOD = 6
NORM_EPS = 1e-6
CONV_DIM = D_MODEL // 2
MLA_HEADS = 8
MLA_NOPE = 128
MLA_ROPE = 64
MLA_V = 128
MLA_QK = MLA_NOPE + MLA_ROPE
MLA_Q_RANK = D_MODEL // 4
MLA_KV_RANK = D_MODEL // 8
ATTN_SCALE = MLA_QK ** -0.5
ROPE_AXIS = MLA_ROPE // 2
ROPE_THETA = 10000.0
GM_DIM = D_MODEL // 2
GM_GROUPS = 8
GM_CHUNK = 128
HY_DIM = D_MODEL // 2
HY_BANDS = 16
HY_TARGET = 1e-2
HY_FAST = 0.3
HY_SLOW = 1.5
PEER_KEYS = 128
PEER_HEADS = 8
PEER_TOPK = 16
PEER_QDIM = 256
PEER_HALF = PEER_QDIM // 2

LANES = 128
VMEM_LIMIT = 56 * 1024 * 1024


def _cparams(sem):
    return pltpu.CompilerParams(dimension_semantics=sem, vmem_limit_bytes=VMEM_LIMIT)


def _tile(n, target, mult=LANES):
    if n <= target:
        return n
    best = None
    for t in range(mult, target + 1, mult):
        if n % t == 0:
            best = t
    assert best is not None, (n, target, mult)
    return best


def _modvec_kernel(a_ref, w_ref, b_ref, o_ref):
    a = a_ref[...]
    a = a * jax.nn.sigmoid(a)
    o_ref[...] = jnp.dot(a, w_ref[...], preferred_element_type=F32,
                         precision=lax.Precision.HIGHEST) + b_ref[...]


def _mod_vectors(rows, w_mod, b_mod):
    d, n = w_mod.shape
    tn = _tile(n, 1024)
    return pl.pallas_call(
        _modvec_kernel,
        grid=(n // tn,),
        in_specs=[pl.BlockSpec((8, d), lambda j: (0, 0)),
                  pl.BlockSpec((d, tn), lambda j: (0, j)),
                  pl.BlockSpec((1, tn), lambda j: (0, j))],
        out_specs=pl.BlockSpec((8, tn), lambda j: (0, j)),
        out_shape=jax.ShapeDtypeStruct((8, n), F32),
        compiler_params=_cparams(("arbitrary",)),
        name="mod_vectors",
    )(rows, w_mod, b_mod.reshape(1, n))


def _nmm_kernel(x_ref, g_ref, sh_ref, sc_ref, w_ref, *rest, emit_h):
    if emit_h:
        o_ref, ho_ref, h_ref = rest
    else:
        o_ref, h_ref = rest

    @pl.when(pl.program_id(1) == 0)
    def _():
        xf = x_ref[...].astype(F32)
        ms = jnp.mean(xf * xf, axis=-1, keepdims=True)
        y = xf * lax.rsqrt(ms + NORM_EPS) * g_ref[...]
        h = (y * (1.0 + sc_ref[...]) + sh_ref[...]).astype(BF16)
        h_ref[...] = h
        if emit_h:
            ho_ref[...] = h

    o_ref[...] = jnp.dot(h_ref[...], w_ref[...], preferred_element_type=F32).astype(o_ref.dtype)


def _norm_mod_matmul(x, g, shift, scale, w, *, out_dtype=F32, tm=1024, tn=1024, emit_h=False):
    m, k = x.shape
    n = w.shape[1]
    tm = _tile(m, tm, 8)
    tn = _tile(n, tn)
    row = lambda v: v.reshape(1, k).astype(F32)
    out_shape = [jax.ShapeDtypeStruct((m, n), out_dtype)]
    out_specs = [pl.BlockSpec((tm, tn), lambda i, j: (i, j))]
    if emit_h:
        out_shape.append(jax.ShapeDtypeStruct((m, k), BF16))
        out_specs.append(pl.BlockSpec((tm, k), lambda i, j: (i, 0)))
    res = pl.pallas_call(
        functools.partial(_nmm_kernel, emit_h=emit_h),
        grid=(m // tm, n // tn),
        in_specs=[pl.BlockSpec((tm, k), lambda i, j: (i, 0)),
                  pl.BlockSpec((1, k), lambda i, j: (0, 0)),
                  pl.BlockSpec((1, k), lambda i, j: (0, 0)),
                  pl.BlockSpec((1, k), lambda i, j: (0, 0)),
                  pl.BlockSpec((k, tn), lambda i, j: (0, j))],
        out_specs=out_specs,
        out_shape=out_shape,
        scratch_shapes=[pltpu.VMEM((tm, k), BF16)],
        compiler_params=_cparams(("parallel", "arbitrary")),
        name="norm_mod_matmul",
    )(x, row(g), row(shift), row(scale), w)
    return res if emit_h else res[0]


def _mmres_kernel(y_ref, w_ref, x_ref, gate_ref, o_ref):
    o_ref[...] = x_ref[...] + gate_ref[...] * jnp.dot(
        y_ref[...], w_ref[...], preferred_element_type=F32)


def _matmul_residual(y, w, x, gate, *, tm=1024, tn=1024):
    m, k = y.shape
    n = w.shape[1]
    tm = _tile(m, tm, 8)
    tn = _tile(n, tn)
    return pl.pallas_call(
        _mmres_kernel,
        grid=(m // tm, n // tn),
        in_specs=[pl.BlockSpec((tm, k), lambda i, j: (i, 0)),
                  pl.BlockSpec((k, tn), lambda i, j: (0, j)),
                  pl.BlockSpec((tm, tn), lambda i, j: (i, j)),
                  pl.BlockSpec((1, tn), lambda i, j: (0, j))],
        out_specs=pl.BlockSpec((tm, tn), lambda i, j: (i, j)),
        out_shape=jax.ShapeDtypeStruct((m, n), F32),
        compiler_params=_cparams(("parallel", "parallel")),
        name="matmul_residual",
    )(y, w, x, gate.reshape(1, n).astype(F32))


def _flash_kernel(q_ref, kc_ref, vc_ref, k_ref, v_ref, o_ref, s_scr, p_scr, *, tk):
    q = q_ref[0]
    tq = q.shape[0]
    n_chunks = k_ref.shape[1] // tk

    def scores(kblk):
        return lax.dot_general(q, kblk, (((1,), (1,)), ((), ())), preferred_element_type=F32)

    def k_chunk(c):
        return k_ref[0, pl.ds(pl.multiple_of(c * tk, tk), tk), :]

    def v_chunk(c):
        return v_ref[0, pl.ds(pl.multiple_of(c * tk, tk), tk), :]

    def soft(s, m_prev, l_prev):
        m_new = jnp.maximum(m_prev, jnp.max(s, axis=-1, keepdims=True))
        alpha = jnp.exp2(m_prev - m_new)
        p = jnp.exp2(s - m_new)
        return m_new, alpha * l_prev + jnp.sum(p, axis=-1, keepdims=True), alpha, p.astype(BF16)

    def pv(p, vblk, acc, alpha):
        return alpha * acc + jnp.dot(p, vblk, preferred_element_type=F32)

    m = jnp.full((tq, 1), -jnp.inf, F32)
    l = jnp.zeros((tq, 1), F32)
    acc = jnp.zeros((tq, v_ref.shape[2]), F32)
    s_scr[0] = scores(k_chunk(0))
    m, l, alpha, p = soft(scores(kc_ref[0]), m, l)
    acc = pv(p, vc_ref[0], acc, alpha)
    s_scr[1] = scores(k_chunk(1))
    m, l, alpha_e, p = soft(s_scr[0], m, l)
    p_scr[0] = p

    def body(cc, carry):
        m, l, acc, alpha_e = carry
        c = 2 * cc + 1
        s_scr[0] = scores(k_chunk(c + 1))
        m, l, alpha_o, p = soft(s_scr[1], m, l)
        p_scr[1] = p
        acc = pv(p_scr[0], v_chunk(c - 1), acc, alpha_e)
        s_scr[1] = scores(k_chunk(c + 2))
        m, l, alpha_e, p = soft(s_scr[0], m, l)
        p_scr[0] = p
        acc = pv(p_scr[1], v_chunk(c), acc, alpha_o)
        return m, l, acc, alpha_e

    m, l, acc, alpha_e = lax.fori_loop(0, (n_chunks - 2) // 2, body, (m, l, acc, alpha_e))
    m, l, alpha_o, p = soft(s_scr[1], m, l)
    acc = pv(p_scr[0], v_chunk(n_chunks - 2), acc, alpha_e)
    acc = pv(p, v_chunk(n_chunks - 1), acc, alpha_o)
    o_ref[...] = (acc / l).astype(o_ref.dtype)


def _attention(q, k_ctx, v_ctx, k, v, *, tq=512, tk=1024):
    h, l, dqk = q.shape
    lk, lc = k.shape[1], k_ctx.shape[1]
    dv = v.shape[2]
    tq = _tile(l, tq)
    tk = _tile(lk, min(tk, lk // 2))
    assert (lk // tk) % 2 == 0
    head = lambda n, dd: pl.BlockSpec((1, n, dd), lambda hh, i: (hh, 0, 0))
    return pl.pallas_call(
        functools.partial(_flash_kernel, tk=tk),
        grid=(h, l // tq),
        in_specs=[pl.BlockSpec((1, tq, dqk), lambda hh, i: (hh, i, 0)),
                  head(lc, dqk), head(lc, dv), head(lk, dqk), head(lk, dv)],
        out_specs=pl.BlockSpec((tq, dv), lambda hh, i: (i, hh)),
        out_shape=jax.ShapeDtypeStruct((l, h * dv), BF16),
        scratch_shapes=[pltpu.VMEM((2, tq, tk), F32), pltpu.VMEM((2, tq, tk), BF16)],
        compiler_params=_cparams(("parallel", "arbitrary")),
        name="mla_attention",
    )(q, k_ctx, v_ctx, k, v)


_GELU_A = -2.0 * math.sqrt(2.0 / math.pi) * math.log2(math.e)
_GELU_B = _GELU_A * 0.044715


def _gelu_tanh(a):
    return a * (1.0 / (1.0 + jnp.exp2(a * (_GELU_A + _GELU_B * (a * a)))))


def _peer_kernel(h_ref, thr_ref, c_ref, s2_ref, p2_ref, u_ref, v_ref, x_ref, gf_ref, fg_ref,
                 o_ref, *, final_norm):
    j = pl.program_id(1)
    te = u_ref.shape[0]
    n_e1 = te // PEER_KEYS

    @pl.when(j == 0)
    def _():
        o_ref[...] = jnp.zeros(o_ref.shape, F32)

    a_t = lax.dot_general(u_ref[...], h_ref[...], (((1,), (1,)), ((), ())),
                          preferred_element_type=F32)
    blocks = []
    for r in range(n_e1):
        e1 = j * n_e1 + r
        w = None
        for hd in range(PEER_HEADS):
            thr_row = thr_ref[hd, pl.ds(e1, 1), :]
            c_row = c_ref[hd, pl.ds(e1, 1), :]
            term = jnp.where(s2_ref[hd] >= thr_row, p2_ref[hd], 0.0) * c_row
            w = term if w is None else w + term
        blocks.append(w)
    w_t = blocks[0] if n_e1 == 1 else jnp.concatenate(blocks, axis=0)
    wa = (w_t * _gelu_tanh(a_t)).T.astype(BF16)
    o_ref[...] += jnp.dot(wa, v_ref[...], preferred_element_type=F32)

    @pl.when(j == pl.num_programs(1) - 1)
    def _():
        y = x_ref[...] + gf_ref[...] * o_ref[...]
        if final_norm:
            ms = jnp.mean(y * y, axis=-1, keepdims=True)
            y = y * lax.rsqrt(ms + NORM_EPS) * fg_ref[...]
        o_ref[...] = y


def _peer_experts(h, thr_t, c_t, s2_t, p2_t, u_tab, v_tab, x, g_f, final_g, *, final_norm,
                  tq=512, te=512):
    t, d = h.shape
    e = u_tab.shape[0]
    tq = _tile(t, tq)
    sel_spec = pl.BlockSpec((PEER_HEADS, PEER_KEYS, tq), lambda i, j: (0, 0, i))
    return pl.pallas_call(
        functools.partial(_peer_kernel, final_norm=final_norm),
        grid=(t // tq, e // te),
        in_specs=[pl.BlockSpec((tq, d), lambda i, j: (i, 0)),
                  sel_spec, sel_spec, sel_spec, sel_spec,
                  pl.BlockSpec((te, d), lambda i, j: (j, 0)),
                  pl.BlockSpec((te, d), lambda i, j: (j, 0)),
                  pl.BlockSpec((tq, d), lambda i, j: (i, 0)),
                  pl.BlockSpec((1, d), lambda i, j: (0, 0)),
                  pl.BlockSpec((1, d), lambda i, j: (0, 0))],
        out_specs=pl.BlockSpec((tq, d), lambda i, j: (i, 0)),
        out_shape=jax.ShapeDtypeStruct((t, d), F32),
        compiler_params=_cparams(("parallel", "arbitrary")),
        name="peer_experts",
    )(h, thr_t, c_t, s2_t, p2_t, u_tab, v_tab, x, g_f.reshape(1, d).astype(F32),
      final_g.reshape(1, d).astype(F32))


_CAND_PAIRS = [(a, b) for a in range(PEER_TOPK) for b in range(PEER_TOPK // (a + 1))]
_CAND_ROWS = -(-len(_CAND_PAIRS) // 8) * 8


def _top_values(work, n):
    vals = []
    for k in range(n):
        m = jnp.max(work, axis=0, keepdims=True)
        vals.append(m)
        if k + 1 < n:
            work = jnp.where(work >= m, -jnp.inf, work)
    return vals


def _peer_select_kernel(q_ref, keys_ref, thr_ref, c_ref, s2_ref, p2_ref, s_scr, cand_scr):
    tq = q_ref.shape[0]
    for p in range(2):
        s_scr[p] = lax.dot_general(keys_ref[p], q_ref[:, p * PEER_HALF:(p + 1) * PEER_HALF],
                                   (((1,), (1,)), ((), ())), preferred_element_type=F32,
                                   precision=lax.Precision.HIGHEST)
    cand_scr[...] = jnp.full(cand_scr.shape, -jnp.inf, F32)
    for ch in range(tq // LANES):
        sl = slice(ch * LANES, (ch + 1) * LANES)
        s1 = s_scr[0, :, sl]
        s2 = s_scr[1, :, sl]
        top1 = _top_values(s1, PEER_TOPK)
        top2 = _top_values(s2, PEER_TOPK)
        for r, (a, b) in enumerate(_CAND_PAIRS):
            cand_scr[pl.ds(r, 1), :] = top1[a] + top2[b]
        best = _top_values(cand_scr[...], PEER_TOPK + 1)
        tau = 0.5 * (best[PEER_TOPK - 1] + best[PEER_TOPK])
        z = 1.0
        for k in range(1, PEER_TOPK):
            z = z + jnp.exp(best[k] - best[0])
        c = jnp.where(s1 >= top1[PEER_TOPK - 1], jnp.exp(s1 - top1[0]), 0.0) * (1.0 / z)
        p2 = jnp.where(s2 >= top2[PEER_TOPK - 1], jnp.exp(s2 - top2[0]), 0.0)
        thr_ref[0, :, sl] = tau - s1
        c_ref[0, :, sl] = c
        s2_ref[0, :, sl] = s2
        p2_ref[0, :, sl] = p2


def _peer_selection(q, sub_keys, *, tq=512):
    t = q.shape[0]
    tq = _tile(t, tq)
    out = jax.ShapeDtypeStruct((PEER_HEADS, PEER_KEYS, t), F32)
    spec = pl.BlockSpec((1, PEER_KEYS, tq), lambda i, h: (h, 0, i))
    return pl.pallas_call(
        _peer_select_kernel,
        grid=(t // tq, PEER_HEADS),
        in_specs=[pl.BlockSpec((tq, PEER_QDIM), lambda i, h: (i, h)),
                  pl.BlockSpec((2, PEER_KEYS, PEER_HALF), lambda i, h: (0, 0, 0))],
        out_specs=[spec, spec, spec, spec],
        out_shape=[out, out, out, out],
        scratch_shapes=[pltpu.VMEM((2, PEER_KEYS, tq), F32),
                        pltpu.VMEM((_CAND_ROWS, LANES), F32)],
        compiler_params=_cparams(("parallel", "parallel")),
        name="peer_select",
    )(q, sub_keys)


def _peer(x, g, shift, scale, gate, w_q, sub_keys, u_tab, v_tab, final_g, final_norm):
    q, h = _norm_mod_matmul(x, g, shift, scale, w_q.astype(BF16), emit_h=True)
    thr_t, c_t, s2_t, p2_t = _peer_selection(q, sub_keys)
    return _peer_experts(h, thr_t, c_t, s2_t, p2_t, u_tab.astype(BF16), v_tab.astype(BF16),
                         x, gate, final_g, final_norm=final_norm)


def _dwconv3(u, w, b=None):
    up = jnp.pad(u, ((1, 1), (0, 0)))
    y = up[:-2] * w[0] + up[1:-1] * w[1] + up[2:] * w[2]
    return y if b is None else y + b


def _rope_tables(n_rows):
    rr, cc = jnp.meshgrid(jnp.arange(n_rows), jnp.arange(GRID_W), indexing='ij')
    inv = ROPE_THETA ** (-jnp.arange(ROPE_AXIS // 2, dtype=F32) / (ROPE_AXIS // 2))
    ar = rr.reshape(-1, 1).astype(F32) * inv
    ac = cc.reshape(-1, 1).astype(F32) * inv
    ang = jnp.concatenate([ar, ar, ac, ac], axis=-1)
    return jnp.cos(ang), jnp.sin(ang)


def _rope(u, cos, sin):
    us = u.reshape(u.shape[:-1] + (2, 2, ROPE_AXIS // 2))
    rot = jnp.stack([-us[..., 1, :], us[..., 0, :]], axis=-2).reshape(u.shape)
    return u * cos[:, None, :] + rot * sin[:, None, :]


def _mixer_conv_mla(x, ctx, g, sh, sc, mc, gate, cos, sin, w_in, conv_w, q_norm_g, w_uq,
                    kv_norm_g, w_ukv, w_out):
    l, d = x.shape
    c0 = 3 * CONV_DIM
    n_in = w_in.shape[1]
    n_pad = -n_in % 512
    w_in_b = jnp.pad(w_in, ((0, 0), (0, n_pad))).astype(BF16)
    p = _norm_mod_matmul(x, g, sh, sc, w_in_b)
    y_conv = p[:, :CONV_DIM] * _dwconv3(p[:, CONV_DIM:2 * CONV_DIM] * p[:, 2 * CONV_DIM:c0], conv_w)
    q_c = p[:, c0:c0 + MLA_Q_RANK]
    kv_c = p[:, c0 + MLA_Q_RANK:c0 + MLA_Q_RANK + MLA_KV_RANK]
    k_r = p[:, c0 + MLA_Q_RANK + MLA_KV_RANK:n_in]
    zq = jnp.zeros((MLA_Q_RANK,), F32)
    zkv = jnp.zeros((MLA_KV_RANK,), F32)
    w_ukv_b = w_ukv.astype(BF16)
    q = _norm_mod_matmul(q_c, q_norm_g, zq, zq, w_uq.astype(BF16)).reshape(l, MLA_HEADS, MLA_QK)
    q = jnp.concatenate([q[..., :MLA_NOPE], _rope(q[..., MLA_NOPE:], cos, sin)], axis=-1)
    q = jnp.transpose(q * (ATTN_SCALE * math.log2(math.e)), (1, 0, 2)).astype(BF16)
    kv = _norm_mod_matmul(kv_c, kv_norm_g, zkv, zkv, w_ukv_b).reshape(l, MLA_HEADS, MLA_NOPE + MLA_V)
    k_r = _rope(k_r[:, None, :], cos, sin)
    n_c = n_in - (c0 + MLA_Q_RANK)
    w_ctx = jnp.pad(w_in[:, c0 + MLA_Q_RANK:], ((0, 0), (0, -n_c % LANES))).astype(BF16)
    pc = _norm_mod_matmul(ctx, g, mc[:d], mc[d:], w_ctx)
    lc = ctx.shape[0]
    kv_x = _norm_mod_matmul(pc[:, :MLA_KV_RANK], kv_norm_g, zkv, zkv, w_ukv_b).reshape(
        lc, MLA_HEADS, MLA_NOPE + MLA_V)
    k_rx = pc[:, MLA_KV_RANK:n_c][:, None, :]

    def keys(kvm, kr):
        return jnp.concatenate(
            [kvm[..., :MLA_NOPE], jnp.broadcast_to(kr, kvm.shape[:2] + (MLA_ROPE,))], axis=-1)

    heads_first = lambda a: jnp.transpose(a, (1, 0, 2)).astype(BF16)
    y_att = _attention(q, heads_first(keys(kv_x, k_rx)), heads_first(kv_x[..., MLA_NOPE:]),
                       heads_first(keys(kv, k_r)), heads_first(kv[..., MLA_NOPE:]))
    y = jnp.concatenate([y_conv.astype(BF16), y_att], axis=-1)
    return _matmul_residual(y, w_out.astype(BF16), x, gate)


def _gmlp_kernel(u_ref, v_ref, g_ref, ws_ref, b_ref, o_ref, vn_ref):
    v = jax.nn.gelu(v_ref[...])
    ms = jnp.mean(v * v, axis=-1, keepdims=True)
    vn_ref[...] = (v * lax.rsqrt(ms + NORM_EPS) * g_ref[...]).astype(BF16)
    gw = GM_DIM // GM_GROUPS
    for ci in range(u_ref.shape[0] // GM_CHUNK):
        rows = slice(ci * GM_CHUNK, (ci + 1) * GM_CHUNK)
        for gi in range(GM_GROUPS):
            cols = slice(gi * gw, (gi + 1) * gw)
            s = jnp.dot(ws_ref[gi], vn_ref[rows, cols], preferred_element_type=F32) + b_ref[gi]
            o_ref[rows, cols] = (jax.nn.gelu(u_ref[rows, cols]) * s).astype(o_ref.dtype)


def _gmlp(p, gm_norm_g, gm_ws, gm_b, *, tm=512):
    l = p.shape[0]
    tm = _tile(l, tm, GM_CHUNK)
    return pl.pallas_call(
        _gmlp_kernel,
        grid=(l // tm,),
        in_specs=[pl.BlockSpec((tm, GM_DIM), lambda i: (i, 0)),
                  pl.BlockSpec((tm, GM_DIM), lambda i: (i, 1)),
                  pl.BlockSpec((1, GM_DIM), lambda i: (0, 0)),
                  pl.BlockSpec((GM_GROUPS, GM_CHUNK, GM_CHUNK), lambda i: (0, 0, 0)),
                  pl.BlockSpec((GM_GROUPS, GM_CHUNK, 1), lambda i: (0, 0, 0))],
        out_specs=pl.BlockSpec((tm, GM_DIM), lambda i: (i, 0)),
        out_shape=jax.ShapeDtypeStruct((l, GM_DIM), BF16),
        scratch_shapes=[pltpu.VMEM((tm, GM_DIM), BF16)],
        compiler_params=_cparams(("parallel",)),
        name="gmlp",
    )(p, p, gm_norm_g.reshape(1, GM_DIM), gm_ws.astype(BF16), gm_b.reshape(GM_GROUPS, GM_CHUNK, 1))


def _shift_rows(u, prev_row, next_row):
    n = u.shape[0]
    r = lax.broadcasted_iota(jnp.int32, u.shape, 0)
    up = jnp.where(r == 0, prev_row, pltpu.roll(u, 1, 0))
    dn = jnp.where(r == n - 1, next_row, pltpu.roll(u, n - 1, 0))
    return up, dn


def _hyena_pre_kernel(*refs):
    ins, (w_ref, b_ref, x0_ref, g1_ref) = refs[:9], refs[9:]
    i = pl.program_id(0)
    first = i == 0
    last = i == pl.num_programs(0) - 1
    tc = x0_ref.shape[1]
    outs = []
    for part in range(3):
        main, prev, nxt = ins[3 * part:3 * part + 3]
        u = main[...]
        prev_row = jnp.where(first, 0.0, prev[7:8, :])
        next_row = jnp.where(last, 0.0, nxt[0:1, :])
        up, dn = _shift_rows(u, prev_row, next_row)
        w = w_ref[part]
        outs.append(up * w[0:1, :] + u * w[1:2, :] + dn * w[2:3, :] + b_ref[part])
    x0_ref[...] = outs[0]
    g1_ref[...] = outs[1] * outs[2]


def _hyena_pre(p, col0, conv_w, conv_b, *, tm=512, tc=512):
    l = p.shape[0]
    tm = _tile(l, tm, 8)
    nrb = l // 8
    specs = []
    for part in range(3):
        cb = (col0 + part * HY_DIM) // tc
        specs += [
            pl.BlockSpec((tm, tc), lambda i, j, cb=cb: (i, cb + j)),
            pl.BlockSpec((8, tc), lambda i, j, cb=cb: (jnp.maximum(i * (tm // 8) - 1, 0), cb + j)),
            pl.BlockSpec((8, tc), lambda i, j, cb=cb: (jnp.minimum((i + 1) * (tm // 8), nrb - 1), cb + j)),
        ]
    w = jnp.transpose(conv_w.reshape(3, 3, HY_DIM), (1, 0, 2))
    b = conv_b.reshape(3, 1, HY_DIM)
    out = jax.ShapeDtypeStruct((l, HY_DIM), F32)
    ospec = pl.BlockSpec((tm, tc), lambda i, j: (i, j))
    return pl.pallas_call(
        _hyena_pre_kernel,
        grid=(l // tm, HY_DIM // tc),
        in_specs=specs + [pl.BlockSpec((3, 3, tc), lambda i, j: (0, 0, j)),
                          pl.BlockSpec((3, 1, tc), lambda i, j: (0, 0, j))],
        out_specs=[ospec, ospec],
        out_shape=[out, out],
        compiler_params=_cparams(("parallel", "parallel")),
        name="hyena_pre",
    )(*([p] * 9), w, b)


def _hyena_filter_kernel(band_ref, phase_ref, w1_ref, b1_ref, f1_ref, w2_ref, b2_ref, f2_ref,
                         w3_ref, delta_ref, k_ref, sum_ref, *, seq):
    i = pl.program_id(0)
    tm = k_ref.shape[0]
    hi = lax.Precision.HIGHEST
    r = i * tm + lax.broadcasted_iota(jnp.int32, (tm, 1), 0)
    lag = jnp.where(r < seq, r, 2 * seq - r).astype(F32)
    t = lag * (1.0 / (seq - 1))
    w = lag * (2.0 * math.pi / seq)
    lane = lax.broadcasted_iota(jnp.int32, (tm, LANES), 1)
    z = jnp.where(lane == 0, t, jnp.where(lane <= 2 * HY_BANDS,
                                          jnp.cos(band_ref[...] * w + phase_ref[...]), 0.0))
    a = jnp.sin(f1_ref[...] * (jnp.dot(z, w1_ref[...], precision=hi,
                                       preferred_element_type=F32) + b1_ref[...]))
    a = jnp.sin(f2_ref[...] * (jnp.dot(a, w2_ref[...], precision=hi,
                                       preferred_element_type=F32) + b2_ref[...]))
    hf = jnp.dot(a, w3_ref[...], precision=hi, preferred_element_type=F32)
    k = jnp.where(r == seq, 0.0, hf * jnp.exp(-t * delta_ref[...]))
    k_ref[...] = k

    @pl.when(i == 0)
    def _():
        sum_ref[...] = jnp.zeros(sum_ref.shape, F32)

    sum_ref[...] += jnp.sum(jnp.abs(k), axis=0, keepdims=True)


def _hyena_filter(seq, w1, b1, f1, w2, b2, f2, w3, *, tm=512):
    import numpy as np
    n = 2 * seq
    tm = _tile(seq, tm, 8)
    bands = np.linspace(1e-4, HY_BANDS - 1, HY_BANDS)
    band = np.zeros((1, LANES), np.float32)
    phase = np.zeros((1, LANES), np.float32)
    band[0, 1:1 + HY_BANDS] = bands
    band[0, 1 + HY_BANDS:1 + 2 * HY_BANDS] = bands
    phase[0, 1 + HY_BANDS:1 + 2 * HY_BANDS] = 0.5 * math.pi
    deltas = np.abs(np.linspace(math.log(HY_TARGET) / HY_SLOW, math.log(HY_TARGET) / HY_FAST,
                                HY_DIM)).astype(np.float32).reshape(1, HY_DIM)
    emb, ffn = w1.shape
    pad2 = lambda a, rr, cc: jnp.pad(a, ((0, rr - a.shape[0]), (0, cc - a.shape[1])))
    row = lambda v: jnp.pad(v.reshape(1, -1), ((0, 0), (0, LANES - v.shape[0])))
    nhalf = seq // tm
    const = lambda shape: pl.BlockSpec(shape, lambda i: (0,) * len(shape))
    return pl.pallas_call(
        functools.partial(_hyena_filter_kernel, seq=seq),
        grid=(n // tm,),
        in_specs=[const((1, LANES)), const((1, LANES)),
                  const((LANES, LANES)), const((1, LANES)), const((1, LANES)),
                  const((LANES, LANES)), const((1, LANES)), const((1, LANES)),
                  pl.BlockSpec((LANES, HY_DIM), lambda i: (0, i // nhalf)),
                  const((1, HY_DIM))],
        out_specs=[pl.BlockSpec((tm, HY_DIM), lambda i: (i, 0)),
                   pl.BlockSpec((1, HY_DIM), lambda i: (0, 0))],
        out_shape=[jax.ShapeDtypeStruct((n, HY_DIM), F32),
                   jax.ShapeDtypeStruct((1, HY_DIM), F32)],
        compiler_params=_cparams(("arbitrary",)),
        name="hyena_filter",
    )(jnp.asarray(band), jnp.asarray(phase), pad2(w1, LANES, LANES), row(b1), row(f1),
      pad2(w2, LANES, LANES), row(b2), row(f2), pad2(w3, LANES, 2 * HY_DIM), jnp.asarray(deltas))


DFT_N2 = 256


def _dft_tables(n1):
    import numpy as np
    n2 = DFT_N2
    n = n1 * n2
    a1 = 2.0 * np.pi * np.outer(np.arange(n1), np.arange(n1)) / n1
    fwd1 = np.concatenate([np.cos(a1), -np.sin(a1)], axis=0)
    inv1 = np.concatenate([np.cos(a1), -np.sin(a1)], axis=1)[:n1 // 2]
    a2 = 2.0 * np.pi * np.outer(np.arange(n2), np.arange(n2)) / n2
    c2, s2 = np.cos(a2), np.sin(a2)
    fwd2 = np.block([[c2, s2], [-s2, c2]])
    inv2 = np.block([[c2, -s2], [s2, c2]])
    th = 2.0 * np.pi * np.outer(np.arange(n2), np.arange(n1)) / n
    bf = lambda a: jnp.asarray(a, dtype=BF16)
    f3 = lambda a: jnp.asarray(a[..., None], dtype=F32)
    return dict(fwd1=bf(fwd1), inv1=bf(inv1), fwd2=bf(fwd2), inv2=bf(inv2),
                twa_c=f3(np.cos(th)), twa_s=f3(np.sin(th)),
                twc_c=f3(np.cos(th).T / n), twc_s=f3(np.sin(th).T / n))


def _dft_a_kernel(x_ref, f_ref, c_ref, s_ref, re_ref, im_ref, *, n1, width):
    for sg in range(x_ref.shape[1] // width):
        cols = slice(sg * width, (sg + 1) * width)
        r = jnp.dot(f_ref[...], x_ref[:, cols].astype(BF16), preferred_element_type=F32)
        re, im = r[:n1], r[n1:]
        c, s = c_ref[sg], s_ref[sg]
        re_ref[:, cols] = (re * c + im * s).astype(re_ref.dtype)
        im_ref[:, cols] = (im * c - re * s).astype(im_ref.dtype)


def _dft_stage_a(x2, tab, n1, width, *, seg=4):
    k, m = x2.shape
    seg = min(seg, DFT_N2)
    tn = seg * width
    out = jax.ShapeDtypeStruct((n1, m), BF16)
    ospec = pl.BlockSpec((n1, tn), lambda j: (0, j))
    tspec = pl.BlockSpec((seg, n1, 1), lambda j: (j, 0, 0))
    return pl.pallas_call(
        functools.partial(_dft_a_kernel, n1=n1, width=width),
        grid=(m // tn,),
        in_specs=[pl.BlockSpec((k, tn), lambda j: (0, j)),
                  pl.BlockSpec((2 * n1, k), lambda j: (0, 0)), tspec, tspec],
        out_specs=[ospec, ospec],
        out_shape=[out, out],
        compiler_params=_cparams(("parallel",)),
        name="dft_stage_a",
    )(x2, tab["fwd1"][:, :k], tab["twa_c"], tab["twa_s"])


def _dft_c_filter_kernel(re_ref, im_ref, f_ref, inv_ref, kre_ref, kim_ref):
    n2 = DFT_N2
    g = (jnp.dot(f_ref[:, :n2], re_ref[0], preferred_element_type=F32)
         + jnp.dot(f_ref[:, n2:], im_ref[0], preferred_element_type=F32))
    kre_ref[0] = g[:n2] * inv_ref[...]
    kim_ref[0] = g[n2:] * inv_ref[...]


def _dft_c_conv_kernel(re_ref, im_ref, kre_ref, kim_ref, f_ref, fi_ref, c_ref, s_ref,
                       ore_ref, oim_ref):
    n2 = DFT_N2
    g = (jnp.dot(f_ref[:, :n2], re_ref[0], preferred_element_type=F32)
         + jnp.dot(f_ref[:, n2:], im_ref[0], preferred_element_type=F32))
    gre, gim = g[:n2], g[n2:]
    kre, kim = kre_ref[0], kim_ref[0]
    yre = (gre * kre - gim * kim).astype(BF16)
    yim = (gre * kim + gim * kre).astype(BF16)
    b = (jnp.dot(fi_ref[:, :n2], yre, preferred_element_type=F32)
         + jnp.dot(fi_ref[:, n2:], yim, preferred_element_type=F32))
    bre, bim = b[:n2], b[n2:]
    c, s = c_ref[0], s_ref[0]
    ore_ref[0] = (bre * c - bim * s).astype(ore_ref.dtype)
    oim_ref[0] = (bre * s + bim * c).astype(oim_ref.dtype)


def _dft_stage_c_filter(re3, im3, tab, inv_norm, *, tc=512):
    n1, n2, c = re3.shape
    tc = _tile(c, tc)
    blk = pl.BlockSpec((1, n2, tc), lambda i, j: (i, 0, j))
    out = jax.ShapeDtypeStruct((n1, n2, c), F32)
    return pl.pallas_call(
        _dft_c_filter_kernel,
        grid=(n1, c // tc),
        in_specs=[blk, blk, pl.BlockSpec((2 * n2, 2 * n2), lambda i, j: (0, 0)),
                  pl.BlockSpec((1, tc), lambda i, j: (0, j))],
        out_specs=[blk, blk],
        out_shape=[out, out],
        compiler_params=_cparams(("parallel", "parallel")),
        name="dft_stage_c_filter",
    )(re3, im3, tab["fwd2"], inv_norm)


def _dft_stage_c_conv(re3, im3, kre, kim, tab, *, tc=512):
    n1, n2, c = re3.shape
    tc = _tile(c, tc)
    blk = pl.BlockSpec((1, n2, tc), lambda i, j: (i, 0, j))
    mat = pl.BlockSpec((2 * n2, 2 * n2), lambda i, j: (0, 0))
    tw = pl.BlockSpec((1, n2, 1), lambda i, j: (i, 0, 0))
    out = jax.ShapeDtypeStruct((n1, n2, c), BF16)
    return pl.pallas_call(
        _dft_c_conv_kernel,
        grid=(n1, c // tc),
        in_specs=[blk, blk, blk, blk, mat, mat, tw, tw],
        out_specs=[blk, blk],
        out_shape=[out, out],
        compiler_params=_cparams(("parallel", "parallel")),
        name="dft_stage_c_conv",
    )(re3, im3, kre, kim, tab["fwd2"], tab["inv2"], tab["twc_c"], tab["twc_s"])


def _dft_a_inv_kernel(re_ref, im_ref, f_ref, x0_ref, g1_ref, bias_ref, o_ref, *, n1):
    y = (jnp.dot(f_ref[:, :n1], re_ref[...], preferred_element_type=F32)
         + jnp.dot(f_ref[:, n1:], im_ref[...], preferred_element_type=F32))
    o_ref[...] = (x0_ref[...] * (y + bias_ref[...] * g1_ref[...])).astype(o_ref.dtype)


def _dft_stage_a_inv(re2, im2, tab, x0_2, g1_2, bias_row, *, tn=4096):
    n1, m = re2.shape
    tn = _tile(m, tn)
    half = n1 // 2
    blk = pl.BlockSpec((n1, tn), lambda j: (0, j))
    hblk = pl.BlockSpec((half, tn), lambda j: (0, j))
    return pl.pallas_call(
        functools.partial(_dft_a_inv_kernel, n1=n1),
        grid=(m // tn,),
        in_specs=[blk, blk, pl.BlockSpec((half, 2 * n1), lambda j: (0, 0)), hblk, hblk,
                  pl.BlockSpec((1, tn), lambda j: (0, 0))],
        out_specs=hblk,
        out_shape=jax.ShapeDtypeStruct((half, m), BF16),
        compiler_params=_cparams(("parallel",)),
        name="dft_stage_a_inv",
    )(re2, im2, tab["inv1"], x0_2, g1_2, bias_row)


def _hyena(p, col0, conv_w, conv_b, w1, b1, f1, w2, b2, f2, w3, bias):
    l = p.shape[0]
    c = HY_DIM
    n1 = 2 * l // DFT_N2
    tab = _dft_tables(n1)
    x0, g1 = _hyena_pre(p, col0, conv_w, conv_b)
    filt, abs_sum = _hyena_filter(l, w1, b1, f1, w2, b2, f2, w3)
    inv_norm = 1.0 / (abs_sum + 1e-6)
    fre, fim = _dft_stage_a(filt.reshape(n1, DFT_N2 * c), tab, n1, c)
    kre, kim = _dft_stage_c_filter(fre.reshape(n1, DFT_N2, c), fim.reshape(n1, DFT_N2, c), tab,
                                   inv_norm)
    are, aim = _dft_stage_a(g1.reshape(n1 // 2, DFT_N2 * c), tab, n1, c)
    bre, bim = _dft_stage_c_conv(are.reshape(n1, DFT_N2, c), aim.reshape(n1, DFT_N2, c),
                                 kre, kim, tab)
    tn = _tile(DFT_N2 * c, 4096)
    bias_row = jnp.tile(bias.reshape(1, c), (1, tn // c))
    y = _dft_stage_a_inv(bre.reshape(n1, DFT_N2 * c), bim.reshape(n1, DFT_N2 * c), tab,
                         x0.reshape(n1 // 2, DFT_N2 * c), g1.reshape(n1 // 2, DFT_N2 * c),
                         bias_row, tn=tn)
    return y.reshape(l, c)


def _mm2res_kernel(y1_ref, y2_ref, w1_ref, w2_ref, x_ref, gate_ref, o_ref):
    acc = jnp.dot(y1_ref[...], w1_ref[...], preferred_element_type=F32)
    acc += jnp.dot(y2_ref[...], w2_ref[...], preferred_element_type=F32)
    o_ref[...] = x_ref[...] + gate_ref[...] * acc


def _matmul2_residual(y1, y2, w, x, gate, *, tm=1024, tn=1024):
    m, k1 = y1.shape
    k2 = y2.shape[1]
    n = w.shape[1]
    tm = _tile(m, tm, 8)
    tn = _tile(n, tn)
    kb = k1 // k2
    assert k1 == kb * k2
    return pl.pallas_call(
        _mm2res_kernel,
        grid=(m // tm, n // tn),
        in_specs=[pl.BlockSpec((tm, k1), lambda i, j: (i, 0)),
                  pl.BlockSpec((tm, k2), lambda i, j: (i, 0)),
                  pl.BlockSpec((k1, tn), lambda i, j: (0, j)),
                  pl.BlockSpec((k2, tn), lambda i, j: (kb, j)),
                  pl.BlockSpec((tm, tn), lambda i, j: (i, j)),
                  pl.BlockSpec((1, tn), lambda i, j: (0, j))],
        out_specs=pl.BlockSpec((tm, tn), lambda i, j: (i, j)),
        out_shape=jax.ShapeDtypeStruct((m, n), F32),
        compiler_params=_cparams(("parallel", "parallel")),
        name="matmul2_residual",
    )(y1, y2, w, w, x, gate.reshape(1, n).astype(F32))


def _mixer_gmlp_hyena(x, g, sh, sc, gate, w_in, gm_norm_g, gm_ws, gm_b, hy_conv_w, hy_conv_b,
                      hy_w1, hy_b1, hy_f1, hy_w2, hy_b2, hy_f2, hy_w3, hy_bias, w_out):
    p = _norm_mod_matmul(x, g, sh, sc, w_in.astype(BF16))
    y_gm = _gmlp(p, gm_norm_g, gm_ws, gm_b)
    y_hy = _hyena(p, 2 * GM_DIM, hy_conv_w, hy_conv_b, hy_w1, hy_b1, hy_f1, hy_w2, hy_b2,
                  hy_f2, hy_w3, hy_bias)
    return _matmul2_residual(y_gm, y_hy, w_out.astype(BF16), x, gate)


def kernel(x, c, ctx, c_ctx, ln1_g0, w_mod0, b_mod0, w_in0, conv_w0, q_norm_g0, w_uq0, kv_norm_g0, w_ukv0, w_out0, ln2_g0, peer_wq0, peer_keys0, peer_u0, peer_v0, ln1_g1, w_mod1, b_mod1, w_in1, gm_norm_g1, gm_ws1, gm_b1, hy_conv_w1, hy_conv_b1, hy_w1, hy_b1, hy_f1, hy_w2, hy_b2, hy_f2, hy_w3, hy_bias1, w_out1, ln2_g1, peer_wq1, peer_keys1, peer_u1, peer_v1, final_g):
    b, l, d = x.shape
    assert b == 1 and d == D_MODEL
    xt = x.reshape(l, d)
    cos, sin = _rope_tables(l // GRID_W)
    rows = jnp.zeros((8, d), F32).at[0].set(c[0]).at[1].set(c_ctx)

    mod0 = _mod_vectors(rows, w_mod0, b_mod0)
    sh_m, sc_m, g_m, sh_f, sc_f, g_f = [mod0[0, j * d:(j + 1) * d] for j in range(N_MOD)]
    mc = mod0[1, :2 * d]
    xt = _mixer_conv_mla(xt, ctx[0], ln1_g0, sh_m, sc_m, mc, g_m, cos, sin, w_in0, conv_w0,
                         q_norm_g0, w_uq0, kv_norm_g0, w_ukv0, w_out0)
    xt = _peer(xt, ln2_g0, sh_f, sc_f, g_f, peer_wq0, peer_keys0, peer_u0, peer_v0, final_g,
               final_norm=False)

    mod1 = _mod_vectors(rows, w_mod1, b_mod1)
    sh_m, sc_m, g_m, sh_f, sc_f, g_f = [mod1[0, j * d:(j + 1) * d] for j in range(N_MOD)]
    xt = _mixer_gmlp_hyena(xt, ln1_g1, sh_m, sc_m, g_m, w_in1, gm_norm_g1, gm_ws1, gm_b1,
                           hy_conv_w1, hy_conv_b1, hy_w1, hy_b1, hy_f1, hy_w2, hy_b2, hy_f2,
                           hy_w3, hy_bias1, w_out1)
    xt = _peer(xt, ln2_g1, sh_f, sc_f, g_f, peer_wq1, peer_keys1, peer_u1, peer_v1, final_g,
               final_norm=True)
    return xt.reshape(b, l, d)
```

```python
import functools
import math

import jax
import jax.numpy as jnp
from jax import lax
from jax.experimental import pallas as pl
from jax.experimental.pallas import tpu as pltpu

F32 = jnp.float32
BF16 = jnp.bfloat16

D_MODEL = 2048
GRID_W = 64
N_MOD = 6
NORM_EPS = 1e-6
CONV_DIM = D_MODEL // 2
MLA_HEADS = 8
MLA_NOPE = 128
MLA_ROPE = 64
MLA_V = 128
MLA_QK = MLA_NOPE + MLA_ROPE
MLA_Q_RANK = D_MODEL // 4
MLA_KV_RANK = D_MODEL // 8
ATTN_SCALE = MLA_QK ** -0.5
ROPE_AXIS = MLA_ROPE // 2
ROPE_THETA = 10000.0
GM_DIM = D_MODEL // 2
GM_GROUPS = 8
GM_CHUNK = 128
HY_DIM = D_MODEL // 2
HY_BANDS = 16
HY_TARGET = 1e-2
HY_FAST = 0.3
HY_SLOW = 1.5
PEER_KEYS = 128
PEER_HEADS = 8
PEER_TOPK = 16
PEER_QDIM = 256
PEER_HALF = PEER_QDIM // 2

LANES = 128
VMEM_LIMIT = 56 * 1024 * 1024


def _cparams(sem):
    return pltpu.CompilerParams(dimension_semantics=sem, vmem_limit_bytes=VMEM_LIMIT)


def _tile(n, target, mult=LANES):
    if n <= target:
        return n
    best = None
    for t in range(mult, target + 1, mult):
        if n % t == 0:
            best = t
    assert best is not None, (n, target, mult)
    return best


def _modvec_kernel(a_ref, w_ref, b_ref, o_ref):
    a = a_ref[...]
    a = a * jax.nn.sigmoid(a)
    o_ref[...] = jnp.dot(a, w_ref[...], preferred_element_type=F32,
                         precision=lax.Precision.HIGHEST) + b_ref[...]


def _mod_vectors(rows, w_mod, b_mod):
    d, n = w_mod.shape
    tn = _tile(n, 1024)
    return pl.pallas_call(
        _modvec_kernel,
        grid=(n // tn,),
        in_specs=[pl.BlockSpec((8, d), lambda j: (0, 0)),
                  pl.BlockSpec((d, tn), lambda j: (0, j)),
                  pl.BlockSpec((1, tn), lambda j: (0, j))],
        out_specs=pl.BlockSpec((8, tn), lambda j: (0, j)),
        out_shape=jax.ShapeDtypeStruct((8, n), F32),
        compiler_params=_cparams(("arbitrary",)),
        name="mod_vectors",
    )(rows, w_mod, b_mod.reshape(1, n))


def _nmm_kernel(x_ref, g_ref, sh_ref, sc_ref, w_ref, *rest, emit_h):
    if emit_h:
        o_ref, ho_ref, h_ref = rest
    else:
        o_ref, h_ref = rest

    @pl.when(pl.program_id(1) == 0)
    def _():
        xf = x_ref[...].astype(F32)
        ms = jnp.mean(xf * xf, axis=-1, keepdims=True)
        y = xf * lax.rsqrt(ms + NORM_EPS) * g_ref[...]
        h = (y * (1.0 + sc_ref[...]) + sh_ref[...]).astype(BF16)
        h_ref[...] = h
        if emit_h:
            ho_ref[...] = h

    o_ref[...] = jnp.dot(h_ref[...], w_ref[...], preferred_element_type=F32).astype(o_ref.dtype)


def _norm_mod_matmul(x, g, shift, scale, w, *, out_dtype=F32, tm=1024, tn=1024, emit_h=False):
    m, k = x.shape
    n = w.shape[1]
    tm = _tile(m, tm, 8)
    tn = _tile(n, tn)
    row = lambda v: v.reshape(1, k).astype(F32)
    out_shape = [jax.ShapeDtypeStruct((m, n), out_dtype)]
    out_specs = [pl.BlockSpec((tm, tn), lambda i, j: (i, j))]
    if emit_h:
        out_shape.append(jax.ShapeDtypeStruct((m, k), BF16))
        out_specs.append(pl.BlockSpec((tm, k), lambda i, j: (i, 0)))
    res = pl.pallas_call(
        functools.partial(_nmm_kernel, emit_h=emit_h),
        grid=(m // tm, n // tn),
        in_specs=[pl.BlockSpec((tm, k), lambda i, j: (i, 0)),
                  pl.BlockSpec((1, k), lambda i, j: (0, 0)),
                  pl.BlockSpec((1, k), lambda i, j: (0, 0)),
                  pl.BlockSpec((1, k), lambda i, j: (0, 0)),
                  pl.BlockSpec((k, tn), lambda i, j: (0, j))],
        out_specs=out_specs,
        out_shape=out_shape,
        scratch_shapes=[pltpu.VMEM((tm, k), BF16)],
        compiler_params=_cparams(("parallel", "arbitrary")),
        name="norm_mod_matmul",
    )(x, row(g), row(shift), row(scale), w)
    return res if emit_h else res[0]


def _flash_kernel(q_ref, kc_ref, vc_ref, k_ref, v_ref, o_ref, s_scr, p_scr, *, tk):
    q = q_ref[0]
    tq = q.shape[0]
    n_chunks = k_ref.shape[1] // tk

    def scores(kblk):
        return lax.dot_general(q, kblk, (((1,), (1,)), ((), ())), preferred_element_type=F32)

    def k_chunk(c):
        return k_ref[0, pl.ds(pl.multiple_of(c * tk, tk), tk), :]

    def v_chunk(c):
        return v_ref[0, pl.ds(pl.multiple_of(c * tk, tk), tk), :]

    def soft(s, m_prev, l_prev):
        m_new = jnp.maximum(m_prev, jnp.max(s, axis=-1, keepdims=True))
        alpha = jnp.exp2(m_prev - m_new)
        p = jnp.exp2(s - m_new)
        return m_new, alpha * l_prev + jnp.sum(p, axis=-1, keepdims=True), alpha, p.astype(BF16)

    def pv(p, vblk, acc, alpha):
        return alpha * acc + jnp.dot(p, vblk, preferred_element_type=F32)

    m = jnp.full((tq, 1), -jnp.inf, F32)
    l = jnp.zeros((tq, 1), F32)
    acc = jnp.zeros((tq, v_ref.shape[2]), F32)
    s_scr[0] = scores(k_chunk(0))
    m, l, alpha, p = soft(scores(kc_ref[0]), m, l)
    acc = pv(p, vc_ref[0], acc, alpha)
    s_scr[1] = scores(k_chunk(1))
    m, l, alpha_e, p = soft(s_scr[0], m, l)
    p_scr[0] = p

    def body(cc, carry):
        m, l, acc, alpha_e = carry
        c = 2 * cc + 1
        s_scr[0] = scores(k_chunk(c + 1))
        m, l, alpha_o, p = soft(s_scr[1], m, l)
        p_scr[1] = p
        acc = pv(p_scr[0], v_chunk(c - 1), acc, alpha_e)
        s_scr[1] = scores(k_chunk(c + 2))
        m, l, alpha_e, p = soft(s_scr[0], m, l)
        p_scr[0] = p
        acc = pv(p_scr[1], v_chunk(c), acc, alpha_o)
        return m, l, acc, alpha_e

    m, l, acc, alpha_e = lax.fori_loop(0, (n_chunks - 2) // 2, body, (m, l, acc, alpha_e))
    m, l, alpha_o, p = soft(s_scr[1], m, l)
    acc = pv(p_scr[0], v_chunk(n_chunks - 2), acc, alpha_e)
    acc = pv(p, v_chunk(n_chunks - 1), acc, alpha_o)
    o_ref[...] = (acc / l).astype(o_ref.dtype)


def _attention(q, k_ctx, v_ctx, k, v, *, tq=512, tk=1024):
    h, l, dqk = q.shape
    lk, lc = k.shape[1], k_ctx.shape[1]
    dv = v.shape[2]
    tq = _tile(l, tq)
    tk = _tile(lk, min(tk, lk // 2))
    assert (lk // tk) % 2 == 0
    head = lambda n, dd: pl.BlockSpec((1, n, dd), lambda hh, i: (hh, 0, 0))
    return pl.pallas_call(
        functools.partial(_flash_kernel, tk=tk),
        grid=(h, l // tq),
        in_specs=[pl.BlockSpec((1, tq, dqk), lambda hh, i: (hh, i, 0)),
                  head(lc, dqk), head(lc, dv), head(lk, dqk), head(lk, dv)],
        out_specs=pl.BlockSpec((tq, dv), lambda hh, i: (i, hh)),
        out_shape=jax.ShapeDtypeStruct((l, h * dv), BF16),
        scratch_shapes=[pltpu.VMEM((2, tq, tk), F32), pltpu.VMEM((2, tq, tk), BF16)],
        compiler_params=_cparams(("parallel", "arbitrary")),
        name="mla_attention",
    )(q, k_ctx, v_ctx, k, v)


_GELU_A = -2.0 * math.sqrt(2.0 / math.pi) * math.log2(math.e)
_GELU_B = _GELU_A * 0.044715


def _gelu_tanh(a):
    return a * (1.0 / (1.0 + jnp.exp2(a * (_GELU_A + _GELU_B * (a * a)))))


def _peer_kernel(h_ref, thr_ref, c_ref, s2_ref, p2_ref, u_ref, v_ref, x_ref, gf_ref, fg_ref,
                 o_ref, *, final_norm):
    j = pl.program_id(1)
    te = u_ref.shape[0]
    n_e1 = te // PEER_KEYS

    @pl.when(j == 0)
    def _():
        o_ref[...] = jnp.zeros(o_ref.shape, F32)

    a_t = lax.dot_general(u_ref[...], h_ref[...], (((1,), (1,)), ((), ())),
                          preferred_element_type=F32)
    blocks = []
    for r in range(n_e1):
        w = None
        for hd in range(PEER_HEADS):
            thr_row = thr_ref[hd, r:r + 1, :]
            c_row = c_ref[hd, r:r + 1, :]
            term = jnp.where(s2_ref[hd] >= thr_row, p2_ref[hd], 0.0) * c_row
            w = term if w is None else w + term
        blocks.append(w)
    w_t = blocks[0] if n_e1 == 1 else jnp.concatenate(blocks, axis=0)
    wa = (w_t * _gelu_tanh(a_t)).T.astype(BF16)
    o_ref[...] += jnp.dot(wa, v_ref[...], preferred_element_type=F32)

    @pl.when(j == pl.num_programs(1) - 1)
    def _():
        y = x_ref[...] + gf_ref[...] * o_ref[...]
        if final_norm:
            ms = jnp.mean(y * y, axis=-1, keepdims=True)
            y = y * lax.rsqrt(ms + NORM_EPS) * fg_ref[...]
        o_ref[...] = y


def _peer_experts(h, thr_t, c_t, s2_t, p2_t, u_tab, v_tab, x, g_f, final_g, *, final_norm,
                  tq=512, te=1024):
    t, d = h.shape
    e = u_tab.shape[0]
    tq = _tile(t, tq)
    n_e1 = te // PEER_KEYS
    sel_spec = pl.BlockSpec((PEER_HEADS, PEER_KEYS, tq), lambda i, j: (0, 0, i))
    row_spec = pl.BlockSpec((PEER_HEADS, n_e1, tq), lambda i, j: (0, j, i))
    return pl.pallas_call(
        functools.partial(_peer_kernel, final_norm=final_norm),
        grid=(t // tq, e // te),
        in_specs=[pl.BlockSpec((tq, d), lambda i, j: (i, 0)),
                  row_spec, row_spec, sel_spec, sel_spec,
                  pl.BlockSpec((te, d), lambda i, j: (j, 0)),
                  pl.BlockSpec((te, d), lambda i, j: (j, 0)),
                  pl.BlockSpec((tq, d), lambda i, j: (i, 0)),
                  pl.BlockSpec((1, d), lambda i, j: (0, 0)),
                  pl.BlockSpec((1, d), lambda i, j: (0, 0))],
        out_specs=pl.BlockSpec((tq, d), lambda i, j: (i, 0)),
        out_shape=jax.ShapeDtypeStruct((t, d), F32),
        compiler_params=_cparams(("parallel", "arbitrary")),
        name="peer_experts",
    )(h, thr_t, c_t, s2_t, p2_t, u_tab, v_tab, x, g_f.reshape(1, d).astype(F32),
      final_g.reshape(1, d).astype(F32))


_CAND_PAIRS = [(a, b) for a in range(PEER_TOPK) for b in range(PEER_TOPK // (a + 1))]
_CAND_ROWS = -(-len(_CAND_PAIRS) // 8) * 8


def _top_values(work, n):
    vals = []
    for k in range(n):
        m = jnp.max(work, axis=0, keepdims=True)
        vals.append(m)
        if k + 1 < n:
            work = jnp.where(work >= m, -jnp.inf, work)
    return vals


def _peer_select_kernel(q_ref, keys_ref, thr_ref, c_ref, s2_ref, p2_ref, s_scr, cand_scr):
    tq = q_ref.shape[0]
    for p in range(2):
        s_scr[p] = lax.dot_general(keys_ref[p], q_ref[:, p * PEER_HALF:(p + 1) * PEER_HALF],
                                   (((1,), (1,)), ((), ())), preferred_element_type=F32,
                                   precision=lax.Precision.HIGHEST)
    cand_scr[...] = jnp.full(cand_scr.shape, -jnp.inf, F32)
    for ch in range(tq // LANES):
        sl = slice(ch * LANES, (ch + 1) * LANES)
        s1 = s_scr[0, :, sl]
        s2 = s_scr[1, :, sl]
        top1 = _top_values(s1, PEER_TOPK)
        top2 = _top_values(s2, PEER_TOPK)
        for r, (a, b) in enumerate(_CAND_PAIRS):
            cand_scr[pl.ds(r, 1), :] = top1[a] + top2[b]
        best = _top_values(cand_scr[...], PEER_TOPK + 1)
        tau = 0.5 * (best[PEER_TOPK - 1] + best[PEER_TOPK])
        z = 1.0
        for k in range(1, PEER_TOPK):
            z = z + jnp.exp(best[k] - best[0])
        c = jnp.where(s1 >= top1[PEER_TOPK - 1], jnp.exp(s1 - top1[0]), 0.0) * (1.0 / z)
        p2 = jnp.where(s2 >= top2[PEER_TOPK - 1], jnp.exp(s2 - top2[0]), 0.0)
        thr_ref[0, :, sl] = tau - s1
        c_ref[0, :, sl] = c
        s2_ref[0, :, sl] = s2
        p2_ref[0, :, sl] = p2


def _peer_selection(q, sub_keys, *, tq=512):
    t = q.shape[0]
    tq = _tile(t, tq)
    out = jax.ShapeDtypeStruct((PEER_HEADS, PEER_KEYS, t), F32)
    spec = pl.BlockSpec((1, PEER_KEYS, tq), lambda i, h: (h, 0, i))
    return pl.pallas_call(
        _peer_select_kernel,
        grid=(t // tq, PEER_HEADS),
        in_specs=[pl.BlockSpec((tq, PEER_QDIM), lambda i, h: (i, h)),
                  pl.BlockSpec((2, PEER_KEYS, PEER_HALF), lambda i, h: (0, 0, 0))],
        out_specs=[spec, spec, spec, spec],
        out_shape=[out, out, out, out],
        scratch_shapes=[pltpu.VMEM((2, PEER_KEYS, tq), F32),
                        pltpu.VMEM((_CAND_ROWS, LANES), F32)],
        compiler_params=_cparams(("parallel", "parallel")),
        name="peer_select",
    )(q, sub_keys)


def _peer(x, g, shift, scale, gate, w_q, sub_keys, u_tab, v_tab, final_g, final_norm):
    q, h = _norm_mod_matmul(x, g, shift, scale, w_q.astype(BF16), emit_h=True)
    thr_t, c_t, s2_t, p2_t = _peer_selection(q, sub_keys)
    return _peer_experts(h, thr_t, c_t, s2_t, p2_t, u_tab.astype(BF16), v_tab.astype(BF16),
                         x, gate, final_g, final_norm=final_norm)


def _shift_rows(u, prev_row, next_row):
    n = u.shape[0]
    r = lax.broadcasted_iota(jnp.int32, u.shape, 0)
    up = jnp.where(r == 0, prev_row, pltpu.roll(u, 1, 0))
    dn = jnp.where(r == n - 1, next_row, pltpu.roll(u, n - 1, 0))
    return up, dn


def _halo_specs(tm, tc, n_row_blocks8, col):
    return [pl.BlockSpec((tm, tc), lambda i, j: (i, col + j)),
            pl.BlockSpec((8, tc), lambda i, j: (jnp.maximum(i * (tm // 8) - 1, 0), col + j)),
            pl.BlockSpec((8, tc), lambda i, j: (jnp.minimum((i + 1) * (tm // 8), n_row_blocks8 - 1),
                                                col + j))]


def _conv_gate_kernel(b_ref, c_ref, cp_ref, cn_ref, v_ref, vp_ref, vn_ref, w_ref, o_ref):
    i = pl.program_id(0)
    u = c_ref[...] * v_ref[...]
    prev_row = jnp.where(i == 0, 0.0, cp_ref[7:8, :] * vp_ref[7:8, :])
    next_row = jnp.where(i == pl.num_programs(0) - 1, 0.0, cn_ref[0:1, :] * vn_ref[0:1, :])
    up, dn = _shift_rows(u, prev_row, next_row)
    w = w_ref[...]
    o_ref[...] = (b_ref[...] * (up * w[0:1, :] + u * w[1:2, :] + dn * w[2:3, :])).astype(o_ref.dtype)


def _conv_gate(p, conv_w, *, tm=512, tc=512):
    l = p.shape[0]
    tm = _tile(l, tm, 8)
    nb = CONV_DIM // tc
    return pl.pallas_call(
        _conv_gate_kernel,
        grid=(l // tm, nb),
        in_specs=([pl.BlockSpec((tm, tc), lambda i, j: (i, j))]
                  + _halo_specs(tm, tc, l // 8, nb) + _halo_specs(tm, tc, l // 8, 2 * nb)
                  + [pl.BlockSpec((3, tc), lambda i, j: (0, j))]),
        out_specs=pl.BlockSpec((tm, tc), lambda i, j: (i, j)),
        out_shape=jax.ShapeDtypeStruct((l, CONV_DIM), BF16),
        compiler_params=_cparams(("parallel", "parallel")),
        name="conv_gate",
    )(*([p] * 7), conv_w)


def _rope_tables(n_rows):
    rr, cc = jnp.meshgrid(jnp.arange(n_rows), jnp.arange(GRID_W), indexing='ij')
    inv = ROPE_THETA ** (-jnp.arange(ROPE_AXIS // 2, dtype=F32) / (ROPE_AXIS // 2))
    ar = rr.reshape(-1, 1).astype(F32) * inv
    ac = cc.reshape(-1, 1).astype(F32) * inv
    ang = jnp.concatenate([ar, ar, ac, ac], axis=-1)
    pad = ((0, 0), (0, LANES - MLA_ROPE))
    return jnp.pad(jnp.cos(ang), pad), jnp.pad(jnp.sin(ang), pad)


def _rope_lanes(x, cos, sin):
    half = ROPE_AXIS // 2
    lane = lax.broadcasted_iota(jnp.int32, x.shape, 1)
    rot = jnp.where((lane & (ROPE_AXIS - 1)) < half,
                    -pltpu.roll(x, LANES - half, 1), pltpu.roll(x, half, 1))
    return x * cos + rot * sin


def _rms_rows(x, g):
    xf = x.astype(F32)
    return xf * lax.rsqrt(jnp.mean(xf * xf, axis=-1, keepdims=True) + NORM_EPS) * g


def _q_proj_kernel(x_ref, g_ref, w_ref, cos_ref, sin_ref, o_ref, h_ref):
    @pl.when(pl.program_id(1) == 0)
    def _():
        h_ref[...] = _rms_rows(x_ref[...], g_ref[...]).astype(BF16)

    y = jnp.dot(h_ref[...], w_ref[...], preferred_element_type=F32)
    y = jnp.concatenate([y[:, :MLA_NOPE], _rope_lanes(y[:, MLA_NOPE:], cos_ref[...], sin_ref[...])],
                        axis=-1)
    o_ref[0] = (y * (ATTN_SCALE * math.log2(math.e))).astype(o_ref.dtype)


def _q_proj(p, col, g, w_uq, cos, sin, *, tm=512):
    l = p.shape[0]
    tm = _tile(l, tm, 8)
    w = jnp.pad(w_uq.reshape(MLA_Q_RANK, MLA_HEADS, MLA_QK),
                ((0, 0), (0, 0), (0, 2 * LANES - MLA_QK))).reshape(MLA_Q_RANK, -1).astype(BF16)
    return pl.pallas_call(
        _q_proj_kernel,
        grid=(l // tm, MLA_HEADS),
        in_specs=[pl.BlockSpec((tm, MLA_Q_RANK), lambda i, h: (i, col // MLA_Q_RANK)),
                  pl.BlockSpec((1, MLA_Q_RANK), lambda i, h: (0, 0)),
                  pl.BlockSpec((MLA_Q_RANK, 2 * LANES), lambda i, h: (0, h)),
                  pl.BlockSpec((tm, LANES), lambda i, h: (i, 0)),
                  pl.BlockSpec((tm, LANES), lambda i, h: (i, 0))],
        out_specs=pl.BlockSpec((1, tm, 2 * LANES), lambda i, h: (h, i, 0)),
        out_shape=jax.ShapeDtypeStruct((MLA_HEADS, l, 2 * LANES), BF16),
        scratch_shapes=[pltpu.VMEM((tm, MLA_Q_RANK), BF16)],
        compiler_params=_cparams(("parallel", "arbitrary")),
        name="q_proj",
    )(p, g.reshape(1, -1), w, cos, sin)


def _kv_proj_kernel(x_ref, kr_ref, g_ref, w_ref, cos_ref, sin_ref, k_ref, v_ref, h_ref, r_ref):
    @pl.when(pl.program_id(1) == 0)
    def _():
        h_ref[...] = _rms_rows(x_ref[...], g_ref[...]).astype(BF16)
        r_ref[...] = _rope_lanes(kr_ref[...], cos_ref[...], sin_ref[...]).astype(BF16)

    y = jnp.dot(h_ref[...], w_ref[...], preferred_element_type=F32)
    k_ref[0] = jnp.concatenate([y[:, :MLA_NOPE].astype(BF16), r_ref[...]], axis=-1)
    v_ref[0] = y[:, MLA_NOPE:].astype(BF16)


def _kv_proj(p, col_kv, col_kr, g, w_ukv, cos, sin, *, tm=512):
    l = p.shape[0]
    tm = _tile(l, tm, 8)
    return pl.pallas_call(
        _kv_proj_kernel,
        grid=(l // tm, MLA_HEADS),
        in_specs=[pl.BlockSpec((tm, MLA_KV_RANK), lambda i, h: (i, col_kv // MLA_KV_RANK)),
                  pl.BlockSpec((tm, LANES), lambda i, h: (i, col_kr // LANES)),
                  pl.BlockSpec((1, MLA_KV_RANK), lambda i, h: (0, 0)),
                  pl.BlockSpec((MLA_KV_RANK, MLA_NOPE + MLA_V), lambda i, h: (0, h)),
                  pl.BlockSpec((tm, LANES), lambda i, h: (i, 0)),
                  pl.BlockSpec((tm, LANES), lambda i, h: (i, 0))],
        out_specs=[pl.BlockSpec((1, tm, 2 * LANES), lambda i, h: (h, i, 0)),
                   pl.BlockSpec((1, tm, MLA_V), lambda i, h: (h, i, 0))],
        out_shape=[jax.ShapeDtypeStruct((MLA_HEADS, l, 2 * LANES), BF16),
                   jax.ShapeDtypeStruct((MLA_HEADS, l, MLA_V), BF16)],
        scratch_shapes=[pltpu.VMEM((tm, MLA_KV_RANK), BF16), pltpu.VMEM((tm, LANES), BF16)],
        compiler_params=_cparams(("parallel", "arbitrary")),
        name="kv_proj",
    )(p, p, g.reshape(1, -1), w_ukv, cos, sin)


def _mixer_conv_mla(x, ctx, g, sh, sc, mc, gate, cos, sin, w_in, conv_w, q_norm_g, w_uq,
                    kv_norm_g, w_ukv, w_out):
    d = x.shape[1]
    c0 = 3 * CONV_DIM
    n_in = w_in.shape[1]
    p = _norm_mod_matmul(x, g, sh, sc, jnp.pad(w_in, ((0, 0), (0, -n_in % 512))).astype(BF16))
    y_conv = _conv_gate(p, conv_w)
    w_ukv_b = w_ukv.astype(BF16)
    q = _q_proj(p, c0, q_norm_g, w_uq, cos, sin)
    k, v = _kv_proj(p, c0 + MLA_Q_RANK, c0 + MLA_Q_RANK + MLA_KV_RANK, kv_norm_g, w_ukv_b, cos, sin)
    lc = ctx.shape[0]
    n_c = n_in - (c0 + MLA_Q_RANK)
    w_ctx = jnp.pad(w_in[:, c0 + MLA_Q_RANK:], ((0, 0), (0, -n_c % LANES))).astype(BF16)
    pc = _norm_mod_matmul(ctx, g, mc[:d], mc[d:], w_ctx)
    k_x, v_x = _kv_proj(pc, 0, MLA_KV_RANK, kv_norm_g, w_ukv_b,
                        jnp.ones((lc, LANES), F32), jnp.zeros((lc, LANES), F32))
    y_att = _attention(q, k_x, v_x, k, v)
    return _matmul2_residual(y_conv, y_att, w_out.astype(BF16), x, gate)


def _gmlp_kernel(u_ref, v_ref, g_ref, ws_ref, b_ref, o_ref, vn_ref):
    v = jax.nn.gelu(v_ref[...])
    ms = jnp.mean(v * v, axis=-1, keepdims=True)
    vn_ref[...] = (v * lax.rsqrt(ms + NORM_EPS) * g_ref[...]).astype(BF16)
    gw = GM_DIM // GM_GROUPS
    for ci in range(u_ref.shape[0] // GM_CHUNK):
        rows = slice(ci * GM_CHUNK, (ci + 1) * GM_CHUNK)
        for gi in range(GM_GROUPS):
            cols = slice(gi * gw, (gi + 1) * gw)
            s = jnp.dot(ws_ref[gi], vn_ref[rows, cols], preferred_element_type=F32) + b_ref[gi]
            o_ref[rows, cols] = (jax.nn.gelu(u_ref[rows, cols]) * s).astype(o_ref.dtype)


def _gmlp(p, gm_norm_g, gm_ws, gm_b, *, tm=512):
    l = p.shape[0]
    tm = _tile(l, tm, GM_CHUNK)
    return pl.pallas_call(
        _gmlp_kernel,
        grid=(l // tm,),
        in_specs=[pl.BlockSpec((tm, GM_DIM), lambda i: (i, 0)),
                  pl.BlockSpec((tm, GM_DIM), lambda i: (i, 1)),
                  pl.BlockSpec((1, GM_DIM), lambda i: (0, 0)),
                  pl.BlockSpec((GM_GROUPS, GM_CHUNK, GM_CHUNK), lambda i: (0, 0, 0)),
                  pl.BlockSpec((GM_GROUPS, GM_CHUNK, 1), lambda i: (0, 0, 0))],
        out_specs=pl.BlockSpec((tm, GM_DIM), lambda i: (i, 0)),
        out_shape=jax.ShapeDtypeStruct((l, GM_DIM), BF16),
        scratch_shapes=[pltpu.VMEM((tm, GM_DIM), BF16)],
        compiler_params=_cparams(("parallel",)),
        name="gmlp",
    )(p, p, gm_norm_g.reshape(1, GM_DIM), gm_ws.astype(BF16), gm_b.reshape(GM_GROUPS, GM_CHUNK, 1))


def _hyena_pre_kernel(*refs):
    ins, (w_ref, b_ref, x0_ref, g1_ref) = refs[:9], refs[9:]
    i = pl.program_id(0)
    first = i == 0
    last = i == pl.num_programs(0) - 1
    tc = x0_ref.shape[1]
    outs = []
    for part in range(3):
        main, prev, nxt = ins[3 * part:3 * part + 3]
        u = main[...]
        prev_row = jnp.where(first, 0.0, prev[7:8, :])
        next_row = jnp.where(last, 0.0, nxt[0:1, :])
        up, dn = _shift_rows(u, prev_row, next_row)
        w = w_ref[part]
        outs.append(up * w[0:1, :] + u * w[1:2, :] + dn * w[2:3, :] + b_ref[part])
    x0_ref[...] = outs[0]
    g1_ref[...] = outs[1] * outs[2]


def _hyena_pre(p, col0, conv_w, conv_b, *, tm=512, tc=512):
    l = p.shape[0]
    tm = _tile(l, tm, 8)
    nrb = l // 8
    specs = []
    for part in range(3):
        cb = (col0 + part * HY_DIM) // tc
        specs += [
            pl.BlockSpec((tm, tc), lambda i, j, cb=cb: (i, cb + j)),
            pl.BlockSpec((8, tc), lambda i, j, cb=cb: (jnp.maximum(i * (tm // 8) - 1, 0), cb + j)),
            pl.BlockSpec((8, tc), lambda i, j, cb=cb: (jnp.minimum((i + 1) * (tm // 8), nrb - 1), cb + j)),
        ]
    w = jnp.transpose(conv_w.reshape(3, 3, HY_DIM), (1, 0, 2))
    b = conv_b.reshape(3, 1, HY_DIM)
    out = jax.ShapeDtypeStruct((l, HY_DIM), F32)
    ospec = pl.BlockSpec((tm, tc), lambda i, j: (i, j))
    return pl.pallas_call(
        _hyena_pre_kernel,
        grid=(l // tm, HY_DIM // tc),
        in_specs=specs + [pl.BlockSpec((3, 3, tc), lambda i, j: (0, 0, j)),
                          pl.BlockSpec((3, 1, tc), lambda i, j: (0, 0, j))],
        out_specs=[ospec, ospec],
        out_shape=[out, out],
        compiler_params=_cparams(("parallel", "parallel")),
        name="hyena_pre",
    )(*([p] * 9), w, b)


def _hyena_filter_kernel(band_ref, phase_ref, w1_ref, b1_ref, f1_ref, w2_ref, b2_ref, f2_ref,
                         w3_ref, delta_ref, k_ref, sum_ref, *, seq):
    i = pl.program_id(0)
    tm = k_ref.shape[0]
    hi = lax.Precision.HIGHEST
    r = i * tm + lax.broadcasted_iota(jnp.int32, (tm, 1), 0)
    lag = jnp.where(r < seq, r, 2 * seq - r).astype(F32)
    t = lag * (1.0 / (seq - 1))
    w = lag * (2.0 * math.pi / seq)
    lane = lax.broadcasted_iota(jnp.int32, (tm, LANES), 1)
    z = jnp.where(lane == 0, t, jnp.where(lane <= 2 * HY_BANDS,
                                          jnp.cos(band_ref[...] * w + phase_ref[...]), 0.0))
    a = jnp.sin(f1_ref[...] * (jnp.dot(z, w1_ref[...], precision=hi,
                                       preferred_element_type=F32) + b1_ref[...]))
    a = jnp.sin(f2_ref[...] * (jnp.dot(a, w2_ref[...], precision=hi,
                                       preferred_element_type=F32) + b2_ref[...]))
    hf = jnp.dot(a, w3_ref[...], precision=hi, preferred_element_type=F32)
    k = jnp.where(r == seq, 0.0, hf * jnp.exp(-t * delta_ref[...]))
    k_ref[...] = k

    @pl.when(i == 0)
    def _():
        sum_ref[...] = jnp.zeros(sum_ref.shape, F32)

    sum_ref[...] += jnp.sum(jnp.abs(k), axis=0, keepdims=True)


def _hyena_filter(seq, w1, b1, f1, w2, b2, f2, w3, *, tm=512):
    import numpy as np
    n = 2 * seq
    tm = _tile(seq, tm, 8)
    bands = np.linspace(1e-4, HY_BANDS - 1, HY_BANDS)
    band = np.zeros((1, LANES), np.float32)
    phase = np.zeros((1, LANES), np.float32)
    band[0, 1:1 + HY_BANDS] = bands
    band[0, 1 + HY_BANDS:1 + 2 * HY_BANDS] = bands
    phase[0, 1 + HY_BANDS:1 + 2 * HY_BANDS] = 0.5 * math.pi
    deltas = np.abs(np.linspace(math.log(HY_TARGET) / HY_SLOW, math.log(HY_TARGET) / HY_FAST,
                                HY_DIM)).astype(np.float32).reshape(1, HY_DIM)
    emb, ffn = w1.shape
    pad2 = lambda a, rr, cc: jnp.pad(a, ((0, rr - a.shape[0]), (0, cc - a.shape[1])))
    row = lambda v: jnp.pad(v.reshape(1, -1), ((0, 0), (0, LANES - v.shape[0])))
    nhalf = seq // tm
    const = lambda shape: pl.BlockSpec(shape, lambda i: (0,) * len(shape))
    return pl.pallas_call(
        functools.partial(_hyena_filter_kernel, seq=seq),
        grid=(n // tm,),
        in_specs=[const((1, LANES)), const((1, LANES)),
                  const((LANES, LANES)), const((1, LANES)), const((1, LANES)),
                  const((LANES, LANES)), const((1, LANES)), const((1, LANES)),
                  pl.BlockSpec((LANES, HY_DIM), lambda i: (0, i // nhalf)),
                  const((1, HY_DIM))],
        out_specs=[pl.BlockSpec((tm, HY_DIM), lambda i: (i, 0)),
                   pl.BlockSpec((1, HY_DIM), lambda i: (0, 0))],
        out_shape=[jax.ShapeDtypeStruct((n, HY_DIM), F32),
                   jax.ShapeDtypeStruct((1, HY_DIM), F32)],
        compiler_params=_cparams(("arbitrary",)),
        name="hyena_filter",
    )(jnp.asarray(band), jnp.asarray(phase), pad2(w1, LANES, LANES), row(b1), row(f1),
      pad2(w2, LANES, LANES), row(b2), row(f2), pad2(w3, LANES, 2 * HY_DIM), jnp.asarray(deltas))


DFT_N2 = 256


def _dft_tables(n1):
    import numpy as np
    n2 = DFT_N2
    n = n1 * n2
    a1 = 2.0 * np.pi * np.outer(np.arange(n1), np.arange(n1)) / n1
    fwd1 = np.concatenate([np.cos(a1), -np.sin(a1)], axis=0)
    inv1 = np.concatenate([np.cos(a1), -np.sin(a1)], axis=1)[:n1 // 2]
    a2 = 2.0 * np.pi * np.outer(np.arange(n2), np.arange(n2)) / n2
    c2, s2 = np.cos(a2), np.sin(a2)
    fwd2 = np.block([[c2, s2], [-s2, c2]])
    inv2 = np.block([[c2, -s2], [s2, c2]])
    th = 2.0 * np.pi * np.outer(np.arange(n2), np.arange(n1)) / n
    bf = lambda a: jnp.asarray(a, dtype=BF16)
    f3 = lambda a: jnp.asarray(a[..., None], dtype=F32)
    return dict(fwd1=bf(fwd1), inv1=bf(inv1), fwd2=bf(fwd2), inv2=bf(inv2),
                twa_c=f3(np.cos(th)), twa_s=f3(np.sin(th)),
                twc_c=f3(np.cos(th).T / n), twc_s=f3(np.sin(th).T / n))


def _dft_a_kernel(x_ref, f_ref, c_ref, s_ref, re_ref, im_ref, *, n1):
    width = x_ref.shape[2]
    for sg in range(x_ref.shape[1]):
        cols = slice(sg * width, (sg + 1) * width)
        r = jnp.dot(f_ref[...], x_ref[:, sg, :].astype(BF16), preferred_element_type=F32)
        re, im = r[:n1], r[n1:]
        c, s = c_ref[sg], s_ref[sg]
        re_ref[:, cols] = (re * c + im * s).astype(re_ref.dtype)
        im_ref[:, cols] = (im * c - re * s).astype(im_ref.dtype)


def _dft_stage_a(x3, tab, n1, *, seg=8):
    k, n2, width = x3.shape
    tn = seg * width
    out = jax.ShapeDtypeStruct((n1, n2 * width), BF16)
    ospec = pl.BlockSpec((n1, tn), lambda j: (0, j))
    tspec = pl.BlockSpec((seg, n1, 1), lambda j: (j, 0, 0))
    return pl.pallas_call(
        functools.partial(_dft_a_kernel, n1=n1),
        grid=(n2 // seg,),
        in_specs=[pl.BlockSpec((k, seg, width), lambda j: (0, j, 0)),
                  pl.BlockSpec((2 * n1, k), lambda j: (0, 0)), tspec, tspec],
        out_specs=[ospec, ospec],
        out_shape=[out, out],
        compiler_params=_cparams(("parallel",)),
        name="dft_stage_a",
    )(x3, tab["fwd1"][:, :k], tab["twa_c"], tab["twa_s"])


def _dft_c_filter_kernel(re_ref, im_ref, f_ref, inv_ref, kre_ref, kim_ref):
    n2 = DFT_N2
    g = (jnp.dot(f_ref[:, :n2], re_ref[0], preferred_element_type=F32)
         + jnp.dot(f_ref[:, n2:], im_ref[0], preferred_element_type=F32))
    kre_ref[0] = g[:n2] * inv_ref[...]
    kim_ref[0] = g[n2:] * inv_ref[...]


def _dft_c_conv_kernel(re_ref, im_ref, kre_ref, kim_ref, f_ref, fi_ref, c_ref, s_ref,
                       ore_ref, oim_ref):
    n2 = DFT_N2
    g = (jnp.dot(f_ref[:, :n2], re_ref[0], preferred_element_type=F32)
         + jnp.dot(f_ref[:, n2:], im_ref[0], preferred_element_type=F32))
    gre, gim = g[:n2], g[n2:]
    kre, kim = kre_ref[0], kim_ref[0]
    yre = (gre * kre - gim * kim).astype(BF16)
    yim = (gre * kim + gim * kre).astype(BF16)
    b = (jnp.dot(fi_ref[:, :n2], yre, preferred_element_type=F32)
         + jnp.dot(fi_ref[:, n2:], yim, preferred_element_type=F32))
    bre, bim = b[:n2], b[n2:]
    c, s = c_ref[0], s_ref[0]
    ore_ref[0] = (bre * c - bim * s).astype(ore_ref.dtype)
    oim_ref[0] = (bre * s + bim * c).astype(oim_ref.dtype)


def _dft_stage_c_filter(re3, im3, tab, inv_norm, *, tc=512):
    n1, n2, c = re3.shape
    tc = _tile(c, tc)
    blk = pl.BlockSpec((1, n2, tc), lambda i, j: (i, 0, j))
    out = jax.ShapeDtypeStruct((n1, n2, c), F32)
    return pl.pallas_call(
        _dft_c_filter_kernel,
        grid=(n1, c // tc),
        in_specs=[blk, blk, pl.BlockSpec((2 * n2, 2 * n2), lambda i, j: (0, 0)),
                  pl.BlockSpec((1, tc), lambda i, j: (0, j))],
        out_specs=[blk, blk],
        out_shape=[out, out],
        compiler_params=_cparams(("parallel", "parallel")),
        name="dft_stage_c_filter",
    )(re3, im3, tab["fwd2"], inv_norm)


def _dft_stage_c_conv(re3, im3, kre, kim, tab, *, tc=512):
    n1, n2, c = re3.shape
    tc = _tile(c, tc)
    blk = pl.BlockSpec((1, n2, tc), lambda i, j: (i, 0, j))
    mat = pl.BlockSpec((2 * n2, 2 * n2), lambda i, j: (0, 0))
    tw = pl.BlockSpec((1, n2, 1), lambda i, j: (i, 0, 0))
    out = jax.ShapeDtypeStruct((n1, n2, c), BF16)
    return pl.pallas_call(
        _dft_c_conv_kernel,
        grid=(n1, c // tc),
        in_specs=[blk, blk, blk, blk, mat, mat, tw, tw],
        out_specs=[blk, blk],
        out_shape=[out, out],
        compiler_params=_cparams(("parallel", "parallel")),
        name="dft_stage_c_conv",
    )(re3, im3, kre, kim, tab["fwd2"], tab["inv2"], tab["twc_c"], tab["twc_s"])


def _dft_a_inv_kernel(re_ref, im_ref, f_ref, x0_ref, g1_ref, bias_ref, o_ref, *, n1):
    width = o_ref.shape[2]
    for sg in range(o_ref.shape[1]):
        cols = slice(sg * width, (sg + 1) * width)
        y = (jnp.dot(f_ref[:, :n1], re_ref[:, cols], preferred_element_type=F32)
             + jnp.dot(f_ref[:, n1:], im_ref[:, cols], preferred_element_type=F32))
        o_ref[:, sg, :] = x0_ref[:, sg, :] * (y + bias_ref[...] * g1_ref[:, sg, :])


def _dft_stage_a_inv(re2, im2, tab, x0_3, g1_3, bias_row, *, seg=8):
    n1 = re2.shape[0]
    half, n2, width = x0_3.shape
    tn = seg * width
    blk = pl.BlockSpec((n1, tn), lambda j: (0, j))
    hblk = pl.BlockSpec((half, seg, width), lambda j: (0, j, 0))
    return pl.pallas_call(
        functools.partial(_dft_a_inv_kernel, n1=n1),
        grid=(n2 // seg,),
        in_specs=[blk, blk, pl.BlockSpec((half, 2 * n1), lambda j: (0, 0)), hblk, hblk,
                  pl.BlockSpec((1, width), lambda j: (0, 0))],
        out_specs=hblk,
        out_shape=jax.ShapeDtypeStruct((half, n2, width), F32),
        compiler_params=_cparams(("parallel",)),
        name="dft_stage_a_inv",
    )(re2, im2, tab["inv1"], x0_3, g1_3, bias_row)


def _hyena(p, col0, conv_w, conv_b, w1, b1, f1, w2, b2, f2, w3, bias):
    l = p.shape[0]
    c = HY_DIM
    n1 = 2 * l // DFT_N2
    tab = _dft_tables(n1)
    x0, g1 = _hyena_pre(p, col0, conv_w, conv_b)
    filt, abs_sum = _hyena_filter(l, w1, b1, f1, w2, b2, f2, w3)
    inv_norm = 1.0 / (abs_sum + 1e-6)
    fre, fim = _dft_stage_a(filt.reshape(n1, DFT_N2, c), tab, n1)
    kre, kim = _dft_stage_c_filter(fre.reshape(n1, DFT_N2, c), fim.reshape(n1, DFT_N2, c), tab,
                                   inv_norm)
    g1_3 = g1.reshape(n1 // 2, DFT_N2, c)
    are, aim = _dft_stage_a(g1_3, tab, n1)
    bre, bim = _dft_stage_c_conv(are.reshape(n1, DFT_N2, c), aim.reshape(n1, DFT_N2, c),
                                 kre, kim, tab)
    y = _dft_stage_a_inv(bre.reshape(n1, DFT_N2 * c), bim.reshape(n1, DFT_N2 * c), tab,
                         x0.reshape(n1 // 2, DFT_N2, c), g1_3, bias.reshape(1, c))
    return y.reshape(l, c)


def _mm2res_kernel(y1_ref, y2_ref, w1_ref, w2_ref, x_ref, gate_ref, o_ref):
    acc = jnp.dot(y1_ref[...].astype(BF16), w1_ref[...], preferred_element_type=F32)
    acc += jnp.dot(y2_ref[...].astype(BF16), w2_ref[...], preferred_element_type=F32)
    o_ref[...] = x_ref[...] + gate_ref[...] * acc


def _matmul2_residual(y1, y2, w, x, gate, *, tm=1024, tn=1024):
    m, k1 = y1.shape
    k2 = y2.shape[1]
    n = w.shape[1]
    tm = _tile(m, tm, 8)
    tn = _tile(n, tn)
    kb = k1 // k2
    assert k1 == kb * k2
    return pl.pallas_call(
        _mm2res_kernel,
        grid=(m // tm, n // tn),
        in_specs=[pl.BlockSpec((tm, k1), lambda i, j: (i, 0)),
                  pl.BlockSpec((tm, k2), lambda i, j: (i, 0)),
                  pl.BlockSpec((k1, tn), lambda i, j: (0, j)),
                  pl.BlockSpec((k2, tn), lambda i, j: (kb, j)),
                  pl.BlockSpec((tm, tn), lambda i, j: (i, j)),
                  pl.BlockSpec((1, tn), lambda i, j: (0, j))],
        out_specs=pl.BlockSpec((tm, tn), lambda i, j: (i, j)),
        out_shape=jax.ShapeDtypeStruct((m, n), F32),
        compiler_params=_cparams(("parallel", "parallel")),
        name="matmul2_residual",
    )(y1, y2, w, w, x, gate.reshape(1, n).astype(F32))


def _mixer_gmlp_hyena(x, g, sh, sc, gate, w_in, gm_norm_g, gm_ws, gm_b, hy_conv_w, hy_conv_b,
                      hy_w1, hy_b1, hy_f1, hy_w2, hy_b2, hy_f2, hy_w3, hy_bias, w_out):
    p = _norm_mod_matmul(x, g, sh, sc, w_in.astype(BF16))
    y_gm = _gmlp(p, gm_norm_g, gm_ws, gm_b)
    y_hy = _hyena(p, 2 * GM_DIM, hy_conv_w, hy_conv_b, hy_w1, hy_b1, hy_f1, hy_w2, hy_b2,
                  hy_f2, hy_w3, hy_bias)
    return _matmul2_residual(y_gm, y_hy, w_out.astype(BF16), x, gate)


def kernel(x, c, ctx, c_ctx, ln1_g0, w_mod0, b_mod0, w_in0, conv_w0, q_norm_g0, w_uq0, kv_norm_g0, w_ukv0, w_out0, ln2_g0, peer_wq0, peer_keys0, peer_u0, peer_v0, ln1_g1, w_mod1, b_mod1, w_in1, gm_norm_g1, gm_ws1, gm_b1, hy_conv_w1, hy_conv_b1, hy_w1, hy_b1, hy_f1, hy_w2, hy_b2, hy_f2, hy_w3, hy_bias1, w_out1, ln2_g1, peer_wq1, peer_keys1, peer_u1, peer_v1, final_g):
    b, l, d = x.shape
    assert b == 1 and d == D_MODEL
    xt = x.reshape(l, d)
    cos, sin = _rope_tables(l // GRID_W)
    rows = jnp.zeros((8, d), F32).at[0].set(c[0]).at[1].set(c_ctx)

    mod0 = _mod_vectors(rows, w_mod0, b_mod0)
    sh_m, sc_m, g_m, sh_f, sc_f, g_f = [mod0[0, j * d:(j + 1) * d] for j in range(N_MOD)]
    mc = mod0[1, :2 * d]
    xt = _mixer_conv_mla(xt, ctx[0], ln1_g0, sh_m, sc_m, mc, g_m, cos, sin, w_in0, conv_w0,
                         q_norm_g0, w_uq0, kv_norm_g0, w_ukv0, w_out0)
    xt = _peer(xt, ln2_g0, sh_f, sc_f, g_f, peer_wq0, peer_keys0, peer_u0, peer_v0, final_g,
               final_norm=False)

    mod1 = _mod_vectors(rows, w_mod1, b_mod1)
    sh_m, sc_m, g_m, sh_f, sc_f, g_f = [mod1[0, j * d:(j + 1) * d] for j in range(N_MOD)]
    xt = _mixer_gmlp_hyena(xt, ln1_g1, sh_m, sc_m, g_m, w_in1, gm_norm_g1, gm_ws1, gm_b1,
                           hy_conv_w1, hy_conv_b1, hy_w1, hy_b1, hy_f1, hy_w2, hy_b2, hy_f2,
                           hy_w3, hy_bias1, w_out1)
    xt = _peer(xt, ln2_g1, sh_f, sc_f, g_f, peer_wq1, peer_keys1, peer_u1, peer_v1, final_g,
               final_norm=True)
    return xt.reshape(b, l, d)
```

```python
import functools
import math

import jax
import jax.numpy as jnp
from jax import lax
from jax.experimental import pallas as pl
from jax.experimental.pallas import tpu as pltpu

F32 = jnp.float32
BF16 = jnp.bfloat16

D_MODEL = 2048
GRID_W = 64
N_MOD = 6
NORM_EPS = 1e-6
CONV_DIM = D_MODEL // 2
MLA_HEADS = 8
MLA_NOPE = 128
MLA_ROPE = 64
MLA_V = 128
MLA_QK = MLA_NOPE + MLA_ROPE
MLA_Q_RANK = D_MODEL // 4
MLA_KV_RANK = D_MODEL // 8
ATTN_SCALE = MLA_QK ** -0.5
ROPE_AXIS = MLA_ROPE // 2
ROPE_THETA = 10000.0
GM_DIM = D_MODEL // 2
GM_GROUPS = 8
GM_CHUNK = 128
HY_DIM = D_MODEL // 2
HY_BANDS = 16
HY_TARGET = 1e-2
HY_FAST = 0.3
HY_SLOW = 1.5
PEER_KEYS = 128
PEER_HEADS = 8
PEER_TOPK = 16
PEER_QDIM = 256
PEER_HALF = PEER_QDIM // 2

LANES = 128
VMEM_LIMIT = 56 * 1024 * 1024

def _cparams(sem):
    return pltpu.CompilerParams(dimension_semantics=sem, vmem_limit_bytes=VMEM_LIMIT)


def _tile(n, target, mult=LANES):
    if n <= target:
        return n
    best = None
    for t in range(mult, target + 1, mult):
        if n % t == 0:
            best = t
    assert best is not None, (n, target, mult)
    return best


def _modvec_kernel(a_ref, w_ref, b_ref, o_ref):
    a = a_ref[...]
    a = a * jax.nn.sigmoid(a)
    o_ref[...] = jnp.dot(a, w_ref[...], preferred_element_type=F32,
                         precision=lax.Precision.HIGHEST) + b_ref[...]


def _mod_vectors(rows, w_mod, b_mod):
    d, n = w_mod.shape
    tn = _tile(n, 1024)
    return pl.pallas_call(
        _modvec_kernel,
        grid=(n // tn,),
        in_specs=[pl.BlockSpec((8, d), lambda j: (0, 0)),
                  pl.BlockSpec((d, tn), lambda j: (0, j)),
                  pl.BlockSpec((1, tn), lambda j: (0, j))],
        out_specs=pl.BlockSpec((8, tn), lambda j: (0, j)),
        out_shape=jax.ShapeDtypeStruct((8, n), F32),
        compiler_params=_cparams(("arbitrary",)),
        name="mod_vectors",
    )(rows, w_mod, b_mod.reshape(1, n))


def _nmm_kernel(x_ref, g_ref, sh_ref, sc_ref, w_ref, *rest, emit_h):
    if emit_h:
        o_ref, ho_ref, h_ref = rest
    else:
        o_ref, h_ref = rest

    @pl.when(pl.program_id(1) == 0)
    def _():
        xf = x_ref[...].astype(F32)
        ms = jnp.mean(xf * xf, axis=-1, keepdims=True)
        y = xf * lax.rsqrt(ms + NORM_EPS) * g_ref[...]
        h = (y * (1.0 + sc_ref[...]) + sh_ref[...]).astype(BF16)
        h_ref[...] = h
        if emit_h:
            ho_ref[...] = h

    o_ref[...] = jnp.dot(h_ref[...], w_ref[...], preferred_element_type=F32).astype(o_ref.dtype)


def _norm_mod_matmul(x, g, shift, scale, w, *, out_dtype=F32, tm=1024, tn=1024, emit_h=False):
    m, k = x.shape
    n = w.shape[1]
    tm = _tile(m, tm, 8)
    tn = _tile(n, tn)
    row = lambda v: v.reshape(1, k).astype(F32)
    out_shape = [jax.ShapeDtypeStruct((m, n), out_dtype)]
    out_specs = [pl.BlockSpec((tm, tn), lambda i, j: (i, j))]
    if emit_h:
        out_shape.append(jax.ShapeDtypeStruct((m, k), BF16))
        out_specs.append(pl.BlockSpec((tm, k), lambda i, j: (i, 0)))
    res = pl.pallas_call(
        functools.partial(_nmm_kernel, emit_h=emit_h),
        grid=(m // tm, n // tn),
        in_specs=[pl.BlockSpec((tm, k), lambda i, j: (i, 0)),
                  pl.BlockSpec((1, k), lambda i, j: (0, 0)),
                  pl.BlockSpec((1, k), lambda i, j: (0, 0)),
                  pl.BlockSpec((1, k), lambda i, j: (0, 0)),
                  pl.BlockSpec((k, tn), lambda i, j: (0, j))],
        out_specs=out_specs,
        out_shape=out_shape,
        scratch_shapes=[pltpu.VMEM((tm, k), BF16)],
        compiler_params=_cparams(("parallel", "arbitrary")),
        name="norm_mod_matmul",
    )(x, row(g), row(shift), row(scale), w)
    return res if emit_h else res[0]


def _flash_kernel(q_ref, kc_ref, vc_ref, k_ref, v_ref, o_ref, s_scr, p_scr, *, tk):
    q = q_ref[0]
    tq = q.shape[0]
    n_chunks = k_ref.shape[1] // tk

    def scores(kblk):
        return lax.dot_general(q, kblk, (((1,), (1,)), ((), ())), preferred_element_type=F32)

    def k_chunk(c):
        return k_ref[0, pl.ds(pl.multiple_of(c * tk, tk), tk), :]

    def v_chunk(c):
        return v_ref[0, pl.ds(pl.multiple_of(c * tk, tk), tk), :]

    def soft(s, m_prev, l_prev):
        m_new = jnp.maximum(m_prev, jnp.max(s, axis=-1, keepdims=True))
        alpha = jnp.exp2(m_prev - m_new)
        p = jnp.exp2(s - m_new)
        return m_new, alpha * l_prev + jnp.sum(p, axis=-1, keepdims=True), alpha, p.astype(BF16)

    def soft_slot(slot, m_prev, l_prev):
        m_new, l_new, alpha, p = soft(s_scr[slot], m_prev, l_prev)
        p_scr[slot] = p
        return m_new, l_new, alpha

    def pv(p, vblk, acc, alpha):
        return alpha * acc + jnp.dot(p, vblk, preferred_element_type=F32)

    m = jnp.full((tq, 1), -jnp.inf, F32)
    l = jnp.zeros((tq, 1), F32)
    acc = jnp.zeros((tq, v_ref.shape[2]), F32)
    s_scr[0] = scores(k_chunk(0))
    m, l, alpha, p = soft(scores(kc_ref[0]), m, l)
    acc = pv(p, vc_ref[0], acc, alpha)
    s_scr[1] = scores(k_chunk(1))
    m, l, alpha_e = soft_slot(0, m, l)

    def body(cc, carry):
        m, l, acc, alpha_e = carry
        c = 2 * cc + 1
        s_scr[0] = scores(k_chunk(c + 1))
        m, l, alpha_o = soft_slot(1, m, l)
        acc = pv(p_scr[0], v_chunk(c - 1), acc, alpha_e)
        s_scr[1] = scores(k_chunk(c + 2))
        m, l, alpha_e = soft_slot(0, m, l)
        acc = pv(p_scr[1], v_chunk(c), acc, alpha_o)
        return m, l, acc, alpha_e

    m, l, acc, alpha_e = lax.fori_loop(0, (n_chunks - 2) // 2, body, (m, l, acc, alpha_e))
    m, l, alpha_o = soft_slot(1, m, l)
    acc = pv(p_scr[0], v_chunk(n_chunks - 2), acc, alpha_e)
    acc = pv(p_scr[1], v_chunk(n_chunks - 1), acc, alpha_o)
    o_ref[...] = (acc / l).astype(o_ref.dtype)


def _attention(q, k_ctx, v_ctx, k, v, *, tq=512, tk=1024):
    h, l, dqk = q.shape
    lk, lc = k.shape[1], k_ctx.shape[1]
    dv = v.shape[2]
    tq = _tile(l, tq)
    tk = _tile(lk, min(tk, lk // 2))
    assert (lk // tk) % 2 == 0
    head = lambda n, dd: pl.BlockSpec((1, n, dd), lambda hh, i: (hh, 0, 0))
    return pl.pallas_call(
        functools.partial(_flash_kernel, tk=tk),
        grid=(h, l // tq),
        in_specs=[pl.BlockSpec((1, tq, dqk), lambda hh, i: (hh, i, 0)),
                  head(lc, dqk), head(lc, dv), head(lk, dqk), head(lk, dv)],
        out_specs=pl.BlockSpec((tq, dv), lambda hh, i: (i, hh)),
        out_shape=jax.ShapeDtypeStruct((l, h * dv), BF16),
        scratch_shapes=[pltpu.VMEM((2, tq, tk), F32), pltpu.VMEM((2, tq, tk), BF16)],
        compiler_params=_cparams(("parallel", "arbitrary")),
        name="mla_attention",
    )(q, k_ctx, v_ctx, k, v)


_GELU_A = -2.0 * math.sqrt(2.0 / math.pi) * math.log2(math.e)
_GELU_B = _GELU_A * 0.044715


def _gelu_tanh(a):
    return a * (1.0 / (1.0 + jnp.exp2(a * (_GELU_A + _GELU_B * (a * a)))))


def _peer_kernel(h_ref, thr_ref, c_ref, s2_ref, p2_ref, u_ref, v_ref, x_ref, gf_ref, fg_ref,
                 o_ref, *, final_norm):
    j = pl.program_id(1)
    te = u_ref.shape[0]
    n_e1 = te // PEER_KEYS

    @pl.when(j == 0)
    def _():
        o_ref[...] = jnp.zeros(o_ref.shape, F32)

    a = lax.dot_general(h_ref[...], u_ref[...], (((1,), (1,)), ((), ())),
                        preferred_element_type=F32)
    blocks = []
    for r in range(n_e1):
        w = None
        for hd in range(PEER_HEADS):
            thr_row = thr_ref[hd, r:r + 1, :]
            c_row = c_ref[hd, r:r + 1, :]
            term = jnp.where(s2_ref[hd] >= thr_row, p2_ref[hd], 0.0) * c_row
            w = term if w is None else w + term
        blocks.append(w)
    w_t = blocks[0] if n_e1 == 1 else jnp.concatenate(blocks, axis=0)
    wa = (w_t.T * _gelu_tanh(a)).astype(BF16)
    o_ref[...] += jnp.dot(wa, v_ref[...], preferred_element_type=F32)

    @pl.when(j == pl.num_programs(1) - 1)
    def _():
        y = x_ref[...] + gf_ref[...] * o_ref[...]
        if final_norm:
            ms = jnp.mean(y * y, axis=-1, keepdims=True)
            y = y * lax.rsqrt(ms + NORM_EPS) * fg_ref[...]
        o_ref[...] = y


def _peer_experts(h, thr_t, c_t, s2_t, p2_t, u_tab, v_tab, x, g_f, final_g, *, final_norm,
                  tq=512, te=1024):
    t, d = h.shape
    e = u_tab.shape[0]
    tq = _tile(t, tq)
    n_e1 = te // PEER_KEYS
    sel_spec = pl.BlockSpec((PEER_HEADS, PEER_KEYS, tq), lambda i, j: (0, 0, i))
    row_spec = pl.BlockSpec((PEER_HEADS, n_e1, tq), lambda i, j: (0, j, i))
    return pl.pallas_call(
        functools.partial(_peer_kernel, final_norm=final_norm),
        grid=(t // tq, e // te),
        in_specs=[pl.BlockSpec((tq, d), lambda i, j: (i, 0)),
                  row_spec, row_spec, sel_spec, sel_spec,
                  pl.BlockSpec((te, d), lambda i, j: (j, 0)),
                  pl.BlockSpec((te, d), lambda i, j: (j, 0)),
                  pl.BlockSpec((tq, d), lambda i, j: (i, 0)),
                  pl.BlockSpec((1, d), lambda i, j: (0, 0)),
                  pl.BlockSpec((1, d), lambda i, j: (0, 0))],
        out_specs=pl.BlockSpec((tq, d), lambda i, j: (i, 0)),
        out_shape=jax.ShapeDtypeStruct((t, d), F32),
        compiler_params=_cparams(("parallel", "arbitrary")),
        name="peer_experts",
    )(h, thr_t, c_t, s2_t, p2_t, u_tab, v_tab, x, g_f.reshape(1, d).astype(F32),
      final_g.reshape(1, d).astype(F32))


_CAND_PAIRS = [(a, b) for a in range(PEER_TOPK) for b in range(PEER_TOPK // (a + 1))]
_CAND_ROWS = -(-len(_CAND_PAIRS) // 8) * 8


def _top_values(works, n):
    vals = [[] for _ in works]
    for k in range(n):
        ms = [jnp.max(w, axis=0, keepdims=True) for w in works]
        for v, m in zip(vals, ms):
            v.append(m)
        if k + 1 < n:
            works = [jnp.where(w >= m, -jnp.inf, w) for w, m in zip(works, ms)]
    return vals


def _peer_select_kernel(q_ref, keys_ref, thr_ref, c_ref, s2_ref, p2_ref, cand_scr):
    s1, s2 = [lax.dot_general(keys_ref[p], q_ref[:, p * PEER_HALF:(p + 1) * PEER_HALF],
                              (((1,), (1,)), ((), ())), preferred_element_type=F32,
                              precision=lax.Precision.HIGHEST) for p in range(2)]
    top1, top2 = _top_values([s1, s2], PEER_TOPK)
    cand_scr[...] = jnp.full(cand_scr.shape, -jnp.inf, F32)
    for r, (a, b) in enumerate(_CAND_PAIRS):
        cand_scr[pl.ds(r, 1), :] = top1[a] + top2[b]
    best, = _top_values([cand_scr[...]], PEER_TOPK + 1)
    tau = 0.5 * (best[PEER_TOPK - 1] + best[PEER_TOPK])
    z = 1.0
    for k in range(1, PEER_TOPK):
        z = z + jnp.exp(best[k] - best[0])
    thr_ref[0] = tau - s1
    c_ref[0] = jnp.where(s1 >= top1[PEER_TOPK - 1], jnp.exp(s1 - top1[0]), 0.0) * (1.0 / z)
    s2_ref[0] = s2
    p2_ref[0] = jnp.where(s2 >= top2[PEER_TOPK - 1], jnp.exp(s2 - top2[0]), 0.0)


def _peer_selection(q, sub_keys, *, tq=512):
    t = q.shape[0]
    tq = _tile(t, tq)
    out = jax.ShapeDtypeStruct((PEER_HEADS, PEER_KEYS, t), F32)
    spec = pl.BlockSpec((1, PEER_KEYS, tq), lambda i, h: (h, 0, i))
    return pl.pallas_call(
        _peer_select_kernel,
        grid=(t // tq, PEER_HEADS),
        in_specs=[pl.BlockSpec((tq, PEER_QDIM), lambda i, h: (i, h)),
                  pl.BlockSpec((2, PEER_KEYS, PEER_HALF), lambda i, h: (0, 0, 0))],
        out_specs=[spec, spec, spec, spec],
        out_shape=[out, out, out, out],
        scratch_shapes=[pltpu.VMEM((_CAND_ROWS, tq), F32)],
        compiler_params=_cparams(("parallel", "parallel")),
        name="peer_select",
    )(q, sub_keys)


def _peer(x, g, shift, scale, gate, w_q, sub_keys, u_tab, v_tab, final_g, final_norm):
    q, h = _norm_mod_matmul(x, g, shift, scale, w_q.astype(BF16), emit_h=True)
    thr_t, c_t, s2_t, p2_t = _peer_selection(q, sub_keys)
    return _peer_experts(h, thr_t, c_t, s2_t, p2_t, u_tab.astype(BF16), v_tab.astype(BF16),
                         x, gate, final_g, final_norm=final_norm)


def _shift_rows(u, prev_row, next_row):
    n = u.shape[0]
    r = lax.broadcasted_iota(jnp.int32, u.shape, 0)
    up = jnp.where(r == 0, prev_row, pltpu.roll(u, 1, 0))
    dn = jnp.where(r == n - 1, next_row, pltpu.roll(u, n - 1, 0))
    return up, dn


def _halo_specs(tm, tc, n_row_blocks8, col):
    return [pl.BlockSpec((tm, tc), lambda i, j: (i, col + j)),
            pl.BlockSpec((8, tc), lambda i, j: (jnp.maximum(i * (tm // 8) - 1, 0), col + j)),
            pl.BlockSpec((8, tc), lambda i, j: (jnp.minimum((i + 1) * (tm // 8), n_row_blocks8 - 1),
                                                col + j))]


def _conv_gate_kernel(b_ref, c_ref, cp_ref, cn_ref, v_ref, vp_ref, vn_ref, w_ref, o_ref):
    i = pl.program_id(0)
    u = c_ref[...] * v_ref[...]
    prev_row = jnp.where(i == 0, 0.0, cp_ref[7:8, :] * vp_ref[7:8, :])
    next_row = jnp.where(i == pl.num_programs(0) - 1, 0.0, cn_ref[0:1, :] * vn_ref[0:1, :])
    up, dn = _shift_rows(u, prev_row, next_row)
    w = w_ref[...]
    o_ref[...] = (b_ref[...] * (up * w[0:1, :] + u * w[1:2, :] + dn * w[2:3, :])).astype(o_ref.dtype)


def _conv_gate(p, conv_w, *, tm=512, tc=512):
    l = p.shape[0]
    tm = _tile(l, tm, 8)
    nb = CONV_DIM // tc
    return pl.pallas_call(
        _conv_gate_kernel,
        grid=(l // tm, nb),
        in_specs=([pl.BlockSpec((tm, tc), lambda i, j: (i, j))]
                  + _halo_specs(tm, tc, l // 8, nb) + _halo_specs(tm, tc, l // 8, 2 * nb)
                  + [pl.BlockSpec((3, tc), lambda i, j: (0, j))]),
        out_specs=pl.BlockSpec((tm, tc), lambda i, j: (i, j)),
        out_shape=jax.ShapeDtypeStruct((l, CONV_DIM), BF16),
        compiler_params=_cparams(("parallel", "parallel")),
        name="conv_gate",
    )(*([p] * 7), conv_w)


def _rope_tables(n_rows):
    rr, cc = jnp.meshgrid(jnp.arange(n_rows), jnp.arange(GRID_W), indexing='ij')
    inv = ROPE_THETA ** (-jnp.arange(ROPE_AXIS // 2, dtype=F32) / (ROPE_AXIS // 2))
    ar = rr.reshape(-1, 1).astype(F32) * inv
    ac = cc.reshape(-1, 1).astype(F32) * inv
    ang = jnp.concatenate([ar, ar, ac, ac], axis=-1)
    pad = ((0, 0), (0, LANES - MLA_ROPE))
    return jnp.pad(jnp.cos(ang), pad), jnp.pad(jnp.sin(ang), pad)


def _rope_lanes(x, cos, sin):
    half = ROPE_AXIS // 2
    lane = lax.broadcasted_iota(jnp.int32, x.shape, 1)
    rot = jnp.where((lane & (ROPE_AXIS - 1)) < half,
                    -pltpu.roll(x, LANES - half, 1), pltpu.roll(x, half, 1))
    return x * cos + rot * sin


def _rms_rows(x, g):
    xf = x.astype(F32)
    return xf * lax.rsqrt(jnp.mean(xf * xf, axis=-1, keepdims=True) + NORM_EPS) * g


def _q_proj_kernel(x_ref, g_ref, w_ref, cos_ref, sin_ref, o_ref, h_ref):
    @pl.when(pl.program_id(1) == 0)
    def _():
        h_ref[...] = _rms_rows(x_ref[...], g_ref[...]).astype(BF16)

    y = jnp.dot(h_ref[...], w_ref[...], preferred_element_type=F32)
    y = jnp.concatenate([y[:, :MLA_NOPE], _rope_lanes(y[:, MLA_NOPE:], cos_ref[...], sin_ref[...])],
                        axis=-1)
    o_ref[0] = (y * (ATTN_SCALE * math.log2(math.e))).astype(o_ref.dtype)


def _q_proj(p, col, g, w_uq, cos, sin, *, tm=1024):
    l = p.shape[0]
    tm = _tile(l, tm, 8)
    w = jnp.pad(w_uq.reshape(MLA_Q_RANK, MLA_HEADS, MLA_QK),
                ((0, 0), (0, 0), (0, 2 * LANES - MLA_QK))).reshape(MLA_Q_RANK, -1).astype(BF16)
    return pl.pallas_call(
        _q_proj_kernel,
        grid=(l // tm, MLA_HEADS),
        in_specs=[pl.BlockSpec((tm, MLA_Q_RANK), lambda i, h: (i, col // MLA_Q_RANK)),
                  pl.BlockSpec((1, MLA_Q_RANK), lambda i, h: (0, 0)),
                  pl.BlockSpec((MLA_Q_RANK, 2 * LANES), lambda i, h: (0, h)),
                  pl.BlockSpec((tm, LANES), lambda i, h: (i, 0)),
                  pl.BlockSpec((tm, LANES), lambda i, h: (i, 0))],
        out_specs=pl.BlockSpec((1, tm, 2 * LANES), lambda i, h: (h, i, 0)),
        out_shape=jax.ShapeDtypeStruct((MLA_HEADS, l, 2 * LANES), BF16),
        scratch_shapes=[pltpu.VMEM((tm, MLA_Q_RANK), BF16)],
        compiler_params=_cparams(("parallel", "arbitrary")),
        name="q_proj",
    )(p, g.reshape(1, -1), w, cos, sin)


def _kv_proj_kernel(x_ref, kr_ref, g_ref, w_ref, cos_ref, sin_ref, k_ref, v_ref, h_ref, r_ref):
    @pl.when(pl.program_id(1) == 0)
    def _():
        h_ref[...] = _rms_rows(x_ref[...], g_ref[...]).astype(BF16)
        r_ref[...] = _rope_lanes(kr_ref[...], cos_ref[...], sin_ref[...]).astype(BF16)

    y = jnp.dot(h_ref[...], w_ref[...], preferred_element_type=F32)
    k_ref[0] = jnp.concatenate([y[:, :MLA_NOPE].astype(BF16), r_ref[...]], axis=-1)
    v_ref[0] = y[:, MLA_NOPE:].astype(BF16)


def _kv_proj(p, col_kv, col_kr, g, w_ukv, cos, sin, *, tm=1024):
    l = p.shape[0]
    tm = _tile(l, tm, 8)
    return pl.pallas_call(
        _kv_proj_kernel,
        grid=(l // tm, MLA_HEADS),
        in_specs=[pl.BlockSpec((tm, MLA_KV_RANK), lambda i, h: (i, col_kv // MLA_KV_RANK)),
                  pl.BlockSpec((tm, LANES), lambda i, h: (i, col_kr // LANES)),
                  pl.BlockSpec((1, MLA_KV_RANK), lambda i, h: (0, 0)),
                  pl.BlockSpec((MLA_KV_RANK, MLA_NOPE + MLA_V), lambda i, h: (0, h)),
                  pl.BlockSpec((tm, LANES), lambda i, h: (i, 0)),
                  pl.BlockSpec((tm, LANES), lambda i, h: (i, 0))],
        out_specs=[pl.BlockSpec((1, tm, 2 * LANES), lambda i, h: (h, i, 0)),
                   pl.BlockSpec((1, tm, MLA_V), lambda i, h: (h, i, 0))],
        out_shape=[jax.ShapeDtypeStruct((MLA_HEADS, l, 2 * LANES), BF16),
                   jax.ShapeDtypeStruct((MLA_HEADS, l, MLA_V), BF16)],
        scratch_shapes=[pltpu.VMEM((tm, MLA_KV_RANK), BF16), pltpu.VMEM((tm, LANES), BF16)],
        compiler_params=_cparams(("parallel", "arbitrary")),
        name="kv_proj",
    )(p, p, g.reshape(1, -1), w_ukv, cos, sin)


def _mixer_conv_mla(x, ctx, g, sh, sc, mc, gate, cos, sin, w_in, conv_w, q_norm_g, w_uq,
                    kv_norm_g, w_ukv, w_out):
    d = x.shape[1]
    c0 = 3 * CONV_DIM
    n_in = w_in.shape[1]
    p = _norm_mod_matmul(x, g, sh, sc, jnp.pad(w_in, ((0, 0), (0, -n_in % 512))).astype(BF16))
    y_conv = _conv_gate(p, conv_w)
    w_ukv_b = w_ukv.astype(BF16)
    q = _q_proj(p, c0, q_norm_g, w_uq, cos, sin)
    k, v = _kv_proj(p, c0 + MLA_Q_RANK, c0 + MLA_Q_RANK + MLA_KV_RANK, kv_norm_g, w_ukv_b, cos, sin)
    lc = ctx.shape[0]
    n_c = n_in - (c0 + MLA_Q_RANK)
    w_ctx = jnp.pad(w_in[:, c0 + MLA_Q_RANK:], ((0, 0), (0, -n_c % LANES))).astype(BF16)
    pc = _norm_mod_matmul(ctx, g, mc[:d], mc[d:], w_ctx)
    k_x, v_x = _kv_proj(pc, 0, MLA_KV_RANK, kv_norm_g, w_ukv_b,
                        jnp.ones((lc, LANES), F32), jnp.zeros((lc, LANES), F32))
    y_att = _attention(q, k_x, v_x, k, v)
    return _matmul2_residual(y_conv, y_att, w_out.astype(BF16), x, gate)


def _gmlp_kernel(u_ref, v_ref, g_ref, ws_ref, b_ref, o_ref, vn_ref):
    v = jax.nn.gelu(v_ref[...])
    ms = jnp.mean(v * v, axis=-1, keepdims=True)
    vn_ref[...] = (v * lax.rsqrt(ms + NORM_EPS) * g_ref[...]).astype(BF16)
    gw = GM_DIM // GM_GROUPS
    for ci in range(u_ref.shape[0] // GM_CHUNK):
        rows = slice(ci * GM_CHUNK, (ci + 1) * GM_CHUNK)
        for gi in range(GM_GROUPS):
            cols = slice(gi * gw, (gi + 1) * gw)
            s = jnp.dot(ws_ref[gi], vn_ref[rows, cols], preferred_element_type=F32) + b_ref[gi]
            o_ref[rows, cols] = (jax.nn.gelu(u_ref[rows, cols]) * s).astype(o_ref.dtype)


def _gmlp(p, gm_norm_g, gm_ws, gm_b, *, tm=512):
    l = p.shape[0]
    tm = _tile(l, tm, GM_CHUNK)
    return pl.pallas_call(
        _gmlp_kernel,
        grid=(l // tm,),
        in_specs=[pl.BlockSpec((tm, GM_DIM), lambda i: (i, 0)),
                  pl.BlockSpec((tm, GM_DIM), lambda i: (i, 1)),
                  pl.BlockSpec((1, GM_DIM), lambda i: (0, 0)),
                  pl.BlockSpec((GM_GROUPS, GM_CHUNK, GM_CHUNK), lambda i: (0, 0, 0)),
                  pl.BlockSpec((GM_GROUPS, GM_CHUNK, 1), lambda i: (0, 0, 0))],
        out_specs=pl.BlockSpec((tm, GM_DIM), lambda i: (i, 0)),
        out_shape=jax.ShapeDtypeStruct((l, GM_DIM), BF16),
        scratch_shapes=[pltpu.VMEM((tm, GM_DIM), BF16)],
        compiler_params=_cparams(("parallel",)),
        name="gmlp",
    )(p, p, gm_norm_g.reshape(1, GM_DIM), gm_ws.astype(BF16), gm_b.reshape(GM_GROUPS, GM_CHUNK, 1))


def _hyena_pre_kernel(*refs):
    ins, (w_ref, b_ref, x0_ref, g1_ref) = refs[:9], refs[9:]
    i = pl.program_id(0)
    first = i == 0
    last = i == pl.num_programs(0) - 1
    tc = x0_ref.shape[1]
    outs = []
    for part in range(3):
        main, prev, nxt = ins[3 * part:3 * part + 3]
        u = main[...]
        prev_row = jnp.where(first, 0.0, prev[7:8, :])
        next_row = jnp.where(last, 0.0, nxt[0:1, :])
        up, dn = _shift_rows(u, prev_row, next_row)
        w = w_ref[part]
        outs.append(up * w[0:1, :] + u * w[1:2, :] + dn * w[2:3, :] + b_ref[part])
    x0_ref[...] = outs[0]
    g1_ref[...] = outs[1] * outs[2]


def _hyena_pre(p, col0, conv_w, conv_b, *, tm=512, tc=512):
    l = p.shape[0]
    tm = _tile(l, tm, 8)
    nrb = l // 8
    specs = []
    for part in range(3):
        cb = (col0 + part * HY_DIM) // tc
        specs += [
            pl.BlockSpec((tm, tc), lambda i, j, cb=cb: (i, cb + j)),
            pl.BlockSpec((8, tc), lambda i, j, cb=cb: (jnp.maximum(i * (tm // 8) - 1, 0), cb + j)),
            pl.BlockSpec((8, tc), lambda i, j, cb=cb: (jnp.minimum((i + 1) * (tm // 8), nrb - 1), cb + j)),
        ]
    w = jnp.transpose(conv_w.reshape(3, 3, HY_DIM), (1, 0, 2))
    b = conv_b.reshape(3, 1, HY_DIM)
    out = jax.ShapeDtypeStruct((l, HY_DIM), F32)
    ospec = pl.BlockSpec((tm, tc), lambda i, j: (i, j))
    return pl.pallas_call(
        _hyena_pre_kernel,
        grid=(l // tm, HY_DIM // tc),
        in_specs=specs + [pl.BlockSpec((3, 3, tc), lambda i, j: (0, 0, j)),
                          pl.BlockSpec((3, 1, tc), lambda i, j: (0, 0, j))],
        out_specs=[ospec, ospec],
        out_shape=[out, out],
        compiler_params=_cparams(("parallel", "parallel")),
        name="hyena_pre",
    )(*([p] * 9), w, b)


def _hyena_filter_kernel(band_ref, phase_ref, w1_ref, b1_ref, f1_ref, w2_ref, b2_ref, f2_ref,
                         w3_ref, delta_ref, k_ref, sum_ref, *, seq):
    i = pl.program_id(0)
    tm = k_ref.shape[0]
    hi = lax.Precision.HIGHEST
    r = i * tm + lax.broadcasted_iota(jnp.int32, (tm, 1), 0)
    lag = jnp.where(r < seq, r, 2 * seq - r).astype(F32)
    t = lag * (1.0 / (seq - 1))
    w = lag * (2.0 * math.pi / seq)
    lane = lax.broadcasted_iota(jnp.int32, (tm, LANES), 1)
    z = jnp.where(lane == 0, t, jnp.where(lane <= 2 * HY_BANDS,
                                          jnp.cos(band_ref[...] * w + phase_ref[...]), 0.0))
    a = jnp.sin(f1_ref[...] * (jnp.dot(z, w1_ref[...], precision=hi,
                                       preferred_element_type=F32) + b1_ref[...]))
    a = jnp.sin(f2_ref[...] * (jnp.dot(a, w2_ref[...], precision=hi,
                                       preferred_element_type=F32) + b2_ref[...]))
    hf = jnp.dot(a, w3_ref[...], precision=hi, preferred_element_type=F32)
    k = jnp.where(r == seq, 0.0, hf * jnp.exp(-t * delta_ref[...]))
    k_ref[...] = k

    @pl.when(i == 0)
    def _():
        sum_ref[...] = jnp.zeros(sum_ref.shape, F32)

    sum_ref[...] += jnp.sum(jnp.abs(k), axis=0, keepdims=True)


def _hyena_filter(seq, w1, b1, f1, w2, b2, f2, w3, *, tm=512):
    import numpy as np
    n = 2 * seq
    tm = _tile(seq, tm, 8)
    bands = np.linspace(1e-4, HY_BANDS - 1, HY_BANDS)
    band = np.zeros((1, LANES), np.float32)
    phase = np.zeros((1, LANES), np.float32)
    band[0, 1:1 + HY_BANDS] = bands
    band[0, 1 + HY_BANDS:1 + 2 * HY_BANDS] = bands
    phase[0, 1 + HY_BANDS:1 + 2 * HY_BANDS] = 0.5 * math.pi
    deltas = np.abs(np.linspace(math.log(HY_TARGET) / HY_SLOW, math.log(HY_TARGET) / HY_FAST,
                                HY_DIM)).astype(np.float32).reshape(1, HY_DIM)
    emb, ffn = w1.shape
    pad2 = lambda a, rr, cc: jnp.pad(a, ((0, rr - a.shape[0]), (0, cc - a.shape[1])))
    row = lambda v: jnp.pad(v.reshape(1, -1), ((0, 0), (0, LANES - v.shape[0])))
    nhalf = seq // tm
    const = lambda shape: pl.BlockSpec(shape, lambda i: (0,) * len(shape))
    return pl.pallas_call(
        functools.partial(_hyena_filter_kernel, seq=seq),
        grid=(n // tm,),
        in_specs=[const((1, LANES)), const((1, LANES)),
                  const((LANES, LANES)), const((1, LANES)), const((1, LANES)),
                  const((LANES, LANES)), const((1, LANES)), const((1, LANES)),
                  pl.BlockSpec((LANES, HY_DIM), lambda i: (0, i // nhalf)),
                  const((1, HY_DIM))],
        out_specs=[pl.BlockSpec((tm, HY_DIM), lambda i: (i, 0)),
                   pl.BlockSpec((1, HY_DIM), lambda i: (0, 0))],
        out_shape=[jax.ShapeDtypeStruct((n, HY_DIM), F32),
                   jax.ShapeDtypeStruct((1, HY_DIM), F32)],
        compiler_params=_cparams(("arbitrary",)),
        name="hyena_filter",
    )(jnp.asarray(band), jnp.asarray(phase), pad2(w1, LANES, LANES), row(b1), row(f1),
      pad2(w2, LANES, LANES), row(b2), row(f2), pad2(w3, LANES, 2 * HY_DIM), jnp.asarray(deltas))


DFT_N2 = 256


def _dft_tables(n1):
    import numpy as np
    n2 = DFT_N2
    n = n1 * n2
    a1 = 2.0 * np.pi * np.outer(np.arange(n1), np.arange(n1)) / n1
    fwd1 = np.concatenate([np.cos(a1), -np.sin(a1)], axis=0)
    inv1 = np.concatenate([np.cos(a1), -np.sin(a1)], axis=1)[:n1 // 2]
    a2 = 2.0 * np.pi * np.outer(np.arange(n2), np.arange(n2)) / n2
    c2, s2 = np.cos(a2), np.sin(a2)
    fwd2 = np.block([[c2, s2], [-s2, c2]])
    inv2 = np.block([[c2, -s2], [s2, c2]])
    th = 2.0 * np.pi * np.outer(np.arange(n2), np.arange(n1)) / n
    bf = lambda a: jnp.asarray(a, dtype=BF16)
    f3 = lambda a: jnp.asarray(a[..., None], dtype=F32)
    return dict(fwd1=bf(fwd1), inv1=bf(inv1), fwd2=bf(fwd2), inv2=bf(inv2),
                twa_c=f3(np.cos(th)), twa_s=f3(np.sin(th)),
                twc_c=f3(np.cos(th).T / n), twc_s=f3(np.sin(th).T / n))


def _dft_a_kernel(x_ref, f_ref, c_ref, s_ref, re_ref, im_ref, *, n1):
    width = x_ref.shape[2]
    for sg in range(x_ref.shape[1]):
        cols = slice(sg * width, (sg + 1) * width)
        r = jnp.dot(f_ref[...], x_ref[:, sg, :].astype(BF16), preferred_element_type=F32)
        re, im = r[:n1], r[n1:]
        c, s = c_ref[sg], s_ref[sg]
        re_ref[:, cols] = (re * c + im * s).astype(re_ref.dtype)
        im_ref[:, cols] = (im * c - re * s).astype(im_ref.dtype)


def _dft_stage_a(x3, tab, n1, *, seg=8):
    k, n2, width = x3.shape
    tn = seg * width
    out = jax.ShapeDtypeStruct((n1, n2 * width), BF16)
    ospec = pl.BlockSpec((n1, tn), lambda j: (0, j))
    tspec = pl.BlockSpec((seg, n1, 1), lambda j: (j, 0, 0))
    return pl.pallas_call(
        functools.partial(_dft_a_kernel, n1=n1),
        grid=(n2 // seg,),
        in_specs=[pl.BlockSpec((k, seg, width), lambda j: (0, j, 0)),
                  pl.BlockSpec((2 * n1, k), lambda j: (0, 0)), tspec, tspec],
        out_specs=[ospec, ospec],
        out_shape=[out, out],
        compiler_params=_cparams(("parallel",)),
        name="dft_stage_a",
    )(x3, tab["fwd1"][:, :k], tab["twa_c"], tab["twa_s"])


def _dft_c_filter_kernel(re_ref, im_ref, f_ref, inv_ref, kre_ref, kim_ref):
    n2 = DFT_N2
    g = (jnp.dot(f_ref[:, :n2], re_ref[0], preferred_element_type=F32)
         + jnp.dot(f_ref[:, n2:], im_ref[0], preferred_element_type=F32))
    kre_ref[0] = g[:n2] * inv_ref[...]
    kim_ref[0] = g[n2:] * inv_ref[...]


def _dft_c_conv_kernel(re_ref, im_ref, kre_ref, kim_ref, f_ref, fi_ref, c_ref, s_ref,
                       ore_ref, oim_ref):
    n2 = DFT_N2
    g = (jnp.dot(f_ref[:, :n2], re_ref[0], preferred_element_type=F32)
         + jnp.dot(f_ref[:, n2:], im_ref[0], preferred_element_type=F32))
    gre, gim = g[:n2], g[n2:]
    kre, kim = kre_ref[0], kim_ref[0]
    yre = (gre * kre - gim * kim).astype(BF16)
    yim = (gre * kim + gim * kre).astype(BF16)
    b = (jnp.dot(fi_ref[:, :n2], yre, preferred_element_type=F32)
         + jnp.dot(fi_ref[:, n2:], yim, preferred_element_type=F32))
    bre, bim = b[:n2], b[n2:]
    c, s = c_ref[0], s_ref[0]
    ore_ref[0] = (bre * c - bim * s).astype(ore_ref.dtype)
    oim_ref[0] = (bre * s + bim * c).astype(oim_ref.dtype)


def _dft_stage_c_filter(re3, im3, tab, inv_norm, *, tc=512):
    n1, n2, c = re3.shape
    tc = _tile(c, tc)
    blk = pl.BlockSpec((1, n2, tc), lambda i, j: (i, 0, j))
    out = jax.ShapeDtypeStruct((n1, n2, c), F32)
    return pl.pallas_call(
        _dft_c_filter_kernel,
        grid=(n1, c // tc),
        in_specs=[blk, blk, pl.BlockSpec((2 * n2, 2 * n2), lambda i, j: (0, 0)),
                  pl.BlockSpec((1, tc), lambda i, j: (0, j))],
        out_specs=[blk, blk],
        out_shape=[out, out],
        compiler_params=_cparams(("parallel", "parallel")),
        name="dft_stage_c_filter",
    )(re3, im3, tab["fwd2"], inv_norm)


def _dft_stage_c_conv(re3, im3, kre, kim, tab, *, tc=512):
    n1, n2, c = re3.shape
    tc = _tile(c, tc)
    blk = pl.BlockSpec((1, n2, tc), lambda i, j: (i, 0, j))
    mat = pl.BlockSpec((2 * n2, 2 * n2), lambda i, j: (0, 0))
    tw = pl.BlockSpec((1, n2, 1), lambda i, j: (i, 0, 0))
    out = jax.ShapeDtypeStruct((n1, n2, c), BF16)
    return pl.pallas_call(
        _dft_c_conv_kernel,
        grid=(n1, c // tc),
        in_specs=[blk, blk, blk, blk, mat, mat, tw, tw],
        out_specs=[blk, blk],
        out_shape=[out, out],
        compiler_params=_cparams(("parallel", "parallel")),
        name="dft_stage_c_conv",
    )(re3, im3, kre, kim, tab["fwd2"], tab["inv2"], tab["twc_c"], tab["twc_s"])


def _dft_a_inv_kernel(re_ref, im_ref, f_ref, x0_ref, g1_ref, bias_ref, o_ref, *, n1):
    width = o_ref.shape[2]
    for sg in range(o_ref.shape[1]):
        cols = slice(sg * width, (sg + 1) * width)
        y = (jnp.dot(f_ref[:, :n1], re_ref[:, cols], preferred_element_type=F32)
             + jnp.dot(f_ref[:, n1:], im_ref[:, cols], preferred_element_type=F32))
        o_ref[:, sg, :] = x0_ref[:, sg, :] * (y + bias_ref[...] * g1_ref[:, sg, :])


def _dft_stage_a_inv(re2, im2, tab, x0_3, g1_3, bias_row, *, seg=8):
    n1 = re2.shape[0]
    half, n2, width = x0_3.shape
    tn = seg * width
    blk = pl.BlockSpec((n1, tn), lambda j: (0, j))
    hblk = pl.BlockSpec((half, seg, width), lambda j: (0, j, 0))
    return pl.pallas_call(
        functools.partial(_dft_a_inv_kernel, n1=n1),
        grid=(n2 // seg,),
        in_specs=[blk, blk, pl.BlockSpec((half, 2 * n1), lambda j: (0, 0)), hblk, hblk,
                  pl.BlockSpec((1, width), lambda j: (0, 0))],
        out_specs=hblk,
        out_shape=jax.ShapeDtypeStruct((half, n2, width), F32),
        compiler_params=_cparams(("parallel",)),
        name="dft_stage_a_inv",
    )(re2, im2, tab["inv1"], x0_3, g1_3, bias_row)


def _hyena(p, col0, conv_w, conv_b, w1, b1, f1, w2, b2, f2, w3, bias):
    l = p.shape[0]
    c = HY_DIM
    n1 = 2 * l // DFT_N2
    tab = _dft_tables(n1)
    x0, g1 = _hyena_pre(p, col0, conv_w, conv_b)
    filt, abs_sum = _hyena_filter(l, w1, b1, f1, w2, b2, f2, w3)
    inv_norm = 1.0 / (abs_sum + 1e-6)
    fre, fim = _dft_stage_a(filt.reshape(n1, DFT_N2, c), tab, n1)
    kre, kim = _dft_stage_c_filter(fre.reshape(n1, DFT_N2, c), fim.reshape(n1, DFT_N2, c), tab,
                                   inv_norm)
    g1_3 = g1.reshape(n1 // 2, DFT_N2, c)
    are, aim = _dft_stage_a(g1_3, tab, n1)
    bre, bim = _dft_stage_c_conv(are.reshape(n1, DFT_N2, c), aim.reshape(n1, DFT_N2, c),
                                 kre, kim, tab)
    y = _dft_stage_a_inv(bre.reshape(n1, DFT_N2 * c), bim.reshape(n1, DFT_N2 * c), tab,
                         x0.reshape(n1 // 2, DFT_N2, c), g1_3, bias.reshape(1, c))
    return y.reshape(l, c)


def _mm2res_kernel(y1_ref, y2_ref, w1_ref, w2_ref, x_ref, gate_ref, o_ref):
    acc = jnp.dot(y1_ref[...].astype(BF16), w1_ref[...], preferred_element_type=F32)
    acc += jnp.dot(y2_ref[...].astype(BF16), w2_ref[...], preferred_element_type=F32)
    o_ref[...] = x_ref[...] + gate_ref[...] * acc


def _matmul2_residual(y1, y2, w, x, gate, *, tm=1024, tn=1024):
    m, k1 = y1.shape
    k2 = y2.shape[1]
    n = w.shape[1]
    tm = _tile(m, tm, 8)
    tn = _tile(n, tn)
    kb = k1 // k2
    assert k1 == kb * k2
    return pl.pallas_call(
        _mm2res_kernel,
        grid=(m // tm, n // tn),
        in_specs=[pl.BlockSpec((tm, k1), lambda i, j: (i, 0)),
                  pl.BlockSpec((tm, k2), lambda i, j: (i, 0)),
                  pl.BlockSpec((k1, tn), lambda i, j: (0, j)),
                  pl.BlockSpec((k2, tn), lambda i, j: (kb, j)),
                  pl.BlockSpec((tm, tn), lambda i, j: (i, j)),
                  pl.BlockSpec((1, tn), lambda i, j: (0, j))],
        out_specs=pl.BlockSpec((tm, tn), lambda i, j: (i, j)),
        out_shape=jax.ShapeDtypeStruct((m, n), F32),
        compiler_params=_cparams(("parallel", "parallel")),
        name="matmul2_residual",
    )(y1, y2, w, w, x, gate.reshape(1, n).astype(F32))


def _mixer_gmlp_hyena(x, g, sh, sc, gate, w_in, gm_norm_g, gm_ws, gm_b, hy_conv_w, hy_conv_b,
                      hy_w1, hy_b1, hy_f1, hy_w2, hy_b2, hy_f2, hy_w3, hy_bias, w_out):
    p = _norm_mod_matmul(x, g, sh, sc, w_in.astype(BF16))
    y_gm = _gmlp(p, gm_norm_g, gm_ws, gm_b)
    y_hy = _hyena(p, 2 * GM_DIM, hy_conv_w, hy_conv_b, hy_w1, hy_b1, hy_f1, hy_w2, hy_b2,
                  hy_f2, hy_w3, hy_bias)
    return _matmul2_residual(y_gm, y_hy, w_out.astype(BF16), x, gate)


def kernel(x, c, ctx, c_ctx, ln1_g0, w_mod0, b_mod0, w_in0, conv_w0, q_norm_g0, w_uq0, kv_norm_g0, w_ukv0, w_out0, ln2_g0, peer_wq0, peer_keys0, peer_u0, peer_v0, ln1_g1, w_mod1, b_mod1, w_in1, gm_norm_g1, gm_ws1, gm_b1, hy_conv_w1, hy_conv_b1, hy_w1, hy_b1, hy_f1, hy_w2, hy_b2, hy_f2, hy_w3, hy_bias1, w_out1, ln2_g1, peer_wq1, peer_keys1, peer_u1, peer_v1, final_g):
    b, l, d = x.shape
    assert b == 1 and d == D_MODEL
    xt = x.reshape(l, d)
    cos, sin = _rope_tables(l // GRID_W)
    rows = jnp.zeros((8, d), F32).at[0].set(c[0]).at[1].set(c_ctx)

    mod0 = _mod_vectors(rows, w_mod0, b_mod0)
    sh_m, sc_m, g_m, sh_f, sc_f, g_f = [mod0[0, j * d:(j + 1) * d] for j in range(N_MOD)]
    mc = mod0[1, :2 * d]
    xt = _mixer_conv_mla(xt, ctx[0], ln1_g0, sh_m, sc_m, mc, g_m, cos, sin, w_in0, conv_w0,
                         q_norm_g0, w_uq0, kv_norm_g0, w_ukv0, w_out0)
    xt = _peer(xt, ln2_g0, sh_f, sc_f, g_f, peer_wq0, peer_keys0, peer_u0, peer_v0, final_g,
               final_norm=False)

    mod1 = _mod_vectors(rows, w_mod1, b_mod1)
    sh_m, sc_m, g_m, sh_f, sc_f, g_f = [mod1[0, j * d:(j + 1) * d] for j in range(N_MOD)]
    xt = _mixer_gmlp_hyena(xt, ln1_g1, sh_m, sc_m, g_m, w_in1, gm_norm_g1, gm_ws1, gm_b1,
                           hy_conv_w1, hy_conv_b1, hy_w1, hy_b1, hy_f1, hy_w2, hy_b2, hy_f2,
                           hy_w3, hy_bias1, w_out1)
    xt = _peer(xt, ln2_g1, sh_f, sc_f, g_f, peer_wq1, peer_keys1, peer_u1, peer_v1, final_g,
               final_norm=True)
    return xt.reshape(b, l, d)
```

```python
import functools
import math

import jax
import jax.numpy as jnp
from jax import lax
from jax.experimental import pallas as pl
from jax.experimental.pallas import tpu as pltpu

F32 = jnp.float32
BF16 = jnp.bfloat16

D_MODEL = 2048
GRID_W = 64
N_MOD = 6
NORM_EPS = 1e-6
CONV_DIM = D_MODEL // 2
MLA_HEADS = 8
MLA_NOPE = 128
MLA_ROPE = 64
MLA_V = 128
MLA_QK = MLA_NOPE + MLA_ROPE
MLA_Q_RANK = D_MODEL // 4
MLA_KV_RANK = D_MODEL // 8
ATTN_SCALE = MLA_QK ** -0.5
ROPE_AXIS = MLA_ROPE // 2
ROPE_THETA = 10000.0
GM_DIM = D_MODEL // 2
GM_GROUPS = 8
GM_CHUNK = 128
HY_DIM = D_MODEL // 2
HY_BANDS = 16
HY_TARGET = 1e-2
HY_FAST = 0.3
HY_SLOW = 1.5
PEER_KEYS = 128
PEER_HEADS = 8
PEER_TOPK = 16
PEER_QDIM = 256
PEER_HALF = PEER_QDIM // 2

LANES = 128
VMEM_LIMIT = 56 * 1024 * 1024

def _cparams(sem):
    return pltpu.CompilerParams(dimension_semantics=sem, vmem_limit_bytes=VMEM_LIMIT)


def _tile(n, target, mult=LANES):
    if n <= target:
        return n
    best = None
    for t in range(mult, target + 1, mult):
        if n % t == 0:
            best = t
    assert best is not None, (n, target, mult)
    return best


def _modvec_kernel(a_ref, w_ref, b_ref, o_ref):
    a = a_ref[...]
    a = a * jax.nn.sigmoid(a)
    o_ref[...] = jnp.dot(a, w_ref[...], preferred_element_type=F32,
                         precision=lax.Precision.HIGHEST) + b_ref[...]


def _mod_vectors(rows, w_mod, b_mod):
    d, n = w_mod.shape
    tn = _tile(n, 1024)
    return pl.pallas_call(
        _modvec_kernel,
        grid=(n // tn,),
        in_specs=[pl.BlockSpec((8, d), lambda j: (0, 0)),
                  pl.BlockSpec((d, tn), lambda j: (0, j)),
                  pl.BlockSpec((1, tn), lambda j: (0, j))],
        out_specs=pl.BlockSpec((8, tn), lambda j: (0, j)),
        out_shape=jax.ShapeDtypeStruct((8, n), F32),
        compiler_params=_cparams(("arbitrary",)),
        name="mod_vectors",
    )(rows, w_mod, b_mod.reshape(1, n))


def _nmm_kernel(x_ref, g_ref, sh_ref, sc_ref, w_ref, *rest, emit_h):
    if emit_h:
        o_ref, ho_ref, h_ref = rest
    else:
        o_ref, h_ref = rest

    @pl.when(pl.program_id(1) == 0)
    def _():
        xf = x_ref[...].astype(F32)
        ms = jnp.mean(xf * xf, axis=-1, keepdims=True)
        y = xf * lax.rsqrt(ms + NORM_EPS) * g_ref[...]
        h = (y * (1.0 + sc_ref[...]) + sh_ref[...]).astype(BF16)
        h_ref[...] = h
        if emit_h:
            ho_ref[...] = h

    o_ref[...] = jnp.dot(h_ref[...], w_ref[...], preferred_element_type=F32).astype(o_ref.dtype)


def _norm_mod_matmul(x, g, shift, scale, w, *, out_dtype=F32, tm=1024, tn=1024, emit_h=False):
    m, k = x.shape
    n = w.shape[1]
    tm = _tile(m, tm, 8)
    tn = _tile(n, tn)
    row = lambda v: v.reshape(1, k).astype(F32)
    out_shape = [jax.ShapeDtypeStruct((m, n), out_dtype)]
    out_specs = [pl.BlockSpec((tm, tn), lambda i, j: (i, j))]
    if emit_h:
        out_shape.append(jax.ShapeDtypeStruct((m, k), BF16))
        out_specs.append(pl.BlockSpec((tm, k), lambda i, j: (i, 0)))
    res = pl.pallas_call(
        functools.partial(_nmm_kernel, emit_h=emit_h),
        grid=(m // tm, n // tn),
        in_specs=[pl.BlockSpec((tm, k), lambda i, j: (i, 0)),
                  pl.BlockSpec((1, k), lambda i, j: (0, 0)),
                  pl.BlockSpec((1, k), lambda i, j: (0, 0)),
                  pl.BlockSpec((1, k), lambda i, j: (0, 0)),
                  pl.BlockSpec((k, tn), lambda i, j: (0, j))],
        out_specs=out_specs,
        out_shape=out_shape,
        scratch_shapes=[pltpu.VMEM((tm, k), BF16)],
        compiler_params=_cparams(("parallel", "arbitrary")),
        name="norm_mod_matmul",
    )(x, row(g), row(shift), row(scale), w)
    return res if emit_h else res[0]


def _flash_kernel(q_ref, kc_ref, vc_ref, k_ref, v_ref, o_ref, s_scr, p_scr, *, tk):
    q = q_ref[0]
    tq = q.shape[0]
    n_chunks = k_ref.shape[1] // tk

    def scores(kblk):
        return lax.dot_general(q, kblk, (((1,), (1,)), ((), ())), preferred_element_type=F32)

    def k_chunk(c):
        return k_ref[0, pl.ds(pl.multiple_of(c * tk, tk), tk), :]

    def v_chunk(c):
        return v_ref[0, pl.ds(pl.multiple_of(c * tk, tk), tk), :]

    def soft(s, m_prev, l_prev):
        m_new = jnp.maximum(m_prev, jnp.max(s, axis=-1, keepdims=True))
        alpha = jnp.exp2(m_prev - m_new)
        p = jnp.exp2(s - m_new)
        return m_new, alpha * l_prev + jnp.sum(p, axis=-1, keepdims=True), alpha, p.astype(BF16)

    def soft_slot(slot, m_prev, l_prev):
        m_new, l_new, alpha, p = soft(s_scr[slot], m_prev, l_prev)
        p_scr[slot] = p
        return m_new, l_new, alpha

    def pv(p, vblk, acc, alpha):
        return alpha * acc + jnp.dot(p, vblk, preferred_element_type=F32)

    m = jnp.full((tq, 1), -jnp.inf, F32)
    l = jnp.zeros((tq, 1), F32)
    acc = jnp.zeros((tq, v_ref.shape[2]), F32)
    s_scr[0] = scores(k_chunk(0))
    m, l, alpha, p = soft(scores(kc_ref[0]), m, l)
    acc = pv(p, vc_ref[0], acc, alpha)
    s_scr[1] = scores(k_chunk(1))
    m, l, alpha_e = soft_slot(0, m, l)

    def body(cc, carry):
        m, l, acc, alpha_e = carry
        c = 2 * cc + 1
        s_scr[0] = scores(k_chunk(c + 1))
        m, l, alpha_o = soft_slot(1, m, l)
        acc = pv(p_scr[0], v_chunk(c - 1), acc, alpha_e)
        s_scr[1] = scores(k_chunk(c + 2))
        m, l, alpha_e = soft_slot(0, m, l)
        acc = pv(p_scr[1], v_chunk(c), acc, alpha_o)
        return m, l, acc, alpha_e

    carry = (m, l, acc, alpha_e)
    for cc in range((n_chunks - 2) // 2):
        carry = body(cc, carry)
    m, l, acc, alpha_e = carry
    m, l, alpha_o = soft_slot(1, m, l)
    acc = pv(p_scr[0], v_chunk(n_chunks - 2), acc, alpha_e)
    acc = pv(p_scr[1], v_chunk(n_chunks - 1), acc, alpha_o)
    o_ref[...] = (acc / l).astype(o_ref.dtype)


def _attention(q, k_ctx, v_ctx, k, v, *, tq=512, tk=1024):
    h, l, dqk = q.shape
    lk, lc = k.shape[1], k_ctx.shape[1]
    dv = v.shape[2]
    tq = _tile(l, tq)
    tk = _tile(lk, min(tk, lk // 2))
    assert (lk // tk) % 2 == 0
    head = lambda n, dd: pl.BlockSpec((1, n, dd), lambda hh, i: (hh, 0, 0))
    return pl.pallas_call(
        functools.partial(_flash_kernel, tk=tk),
        grid=(h, l // tq),
        in_specs=[pl.BlockSpec((1, tq, dqk), lambda hh, i: (hh, i, 0)),
                  head(lc, dqk), head(lc, dv), head(lk, dqk), head(lk, dv)],
        out_specs=pl.BlockSpec((tq, dv), lambda hh, i: (i, hh)),
        out_shape=jax.ShapeDtypeStruct((l, h * dv), BF16),
        scratch_shapes=[pltpu.VMEM((2, tq, tk), F32), pltpu.VMEM((2, tq, tk), BF16)],
        compiler_params=_cparams(("parallel", "arbitrary")),
        name="mla_attention",
    )(q, k_ctx, v_ctx, k, v)


_GELU_A = -2.0 * math.sqrt(2.0 / math.pi) * math.log2(math.e)
_GELU_B = _GELU_A * 0.044715


def _gelu_tanh(a):
    return a * (1.0 / (1.0 + jnp.exp2(a * (_GELU_A + _GELU_B * (a * a)))))


def _peer_kernel(h_ref, thr_ref, c_ref, s2_ref, p2_ref, u_ref, v_ref, x_ref, gf_ref, fg_ref,
                 o_ref, *, final_norm):
    j = pl.program_id(1)
    te = u_ref.shape[0]
    n_e1 = te // PEER_KEYS

    @pl.when(j == 0)
    def _():
        o_ref[...] = jnp.zeros(o_ref.shape, F32)

    a = lax.dot_general(h_ref[...], u_ref[...], (((1,), (1,)), ((), ())),
                        preferred_element_type=F32)
    blocks = []
    for r in range(n_e1):
        w = None
        for hd in range(PEER_HEADS):
            thr_row = thr_ref[hd, r:r + 1, :]
            c_row = c_ref[hd, r:r + 1, :]
            term = jnp.where(s2_ref[hd] >= thr_row, p2_ref[hd], 0.0) * c_row
            w = term if w is None else w + term
        blocks.append(w)
    w_t = blocks[0] if n_e1 == 1 else jnp.concatenate(blocks, axis=0)
    wa = (w_t.T * _gelu_tanh(a)).astype(BF16)
    o_ref[...] += jnp.dot(wa, v_ref[...], preferred_element_type=F32)

    @pl.when(j == pl.num_programs(1) - 1)
    def _():
        y = x_ref[...] + gf_ref[...] * o_ref[...]
        if final_norm:
            ms = jnp.mean(y * y, axis=-1, keepdims=True)
            y = y * lax.rsqrt(ms + NORM_EPS) * fg_ref[...]
        o_ref[...] = y


def _peer_experts(h, thr_t, c_t, s2_t, p2_t, u_tab, v_tab, x, g_f, final_g, *, final_norm,
                  tq=512, te=1024):
    t, d = h.shape
    e = u_tab.shape[0]
    tq = _tile(t, tq)
    n_e1 = te // PEER_KEYS
    sel_spec = pl.BlockSpec((PEER_HEADS, PEER_KEYS, tq), lambda i, j: (0, 0, i))
    row_spec = pl.BlockSpec((PEER_HEADS, n_e1, tq), lambda i, j: (0, j, i))
    return pl.pallas_call(
        functools.partial(_peer_kernel, final_norm=final_norm),
        grid=(t // tq, e // te),
        in_specs=[pl.BlockSpec((tq, d), lambda i, j: (i, 0)),
                  row_spec, row_spec, sel_spec, sel_spec,
                  pl.BlockSpec((te, d), lambda i, j: (j, 0)),
                  pl.BlockSpec((te, d), lambda i, j: (j, 0)),
                  pl.BlockSpec((tq, d), lambda i, j: (i, 0)),
                  pl.BlockSpec((1, d), lambda i, j: (0, 0)),
                  pl.BlockSpec((1, d), lambda i, j: (0, 0))],
        out_specs=pl.BlockSpec((tq, d), lambda i, j: (i, 0)),
        out_shape=jax.ShapeDtypeStruct((t, d), F32),
        compiler_params=_cparams(("parallel", "arbitrary")),
        name="peer_experts",
    )(h, thr_t, c_t, s2_t, p2_t, u_tab, v_tab, x, g_f.reshape(1, d).astype(F32),
      final_g.reshape(1, d).astype(F32))


_CAND_PAIRS = [(a, b) for a in range(PEER_TOPK) for b in range(PEER_TOPK // (a + 1))]
_CAND_ROWS = -(-len(_CAND_PAIRS) // 8) * 8


def _top_values(works, n):
    vals = [[] for _ in works]
    for k in range(n):
        ms = [jnp.max(w, axis=0, keepdims=True) for w in works]
        for v, m in zip(vals, ms):
            v.append(m)
        if k + 1 < n:
            works = [jnp.where(w >= m, -jnp.inf, w) for w, m in zip(works, ms)]
    return vals


def _peer_select_kernel(q_ref, keys_ref, thr_ref, c_ref, s2_ref, p2_ref, cand_scr):
    s1, s2 = [lax.dot_general(keys_ref[p], q_ref[:, p * PEER_HALF:(p + 1) * PEER_HALF],
                              (((1,), (1,)), ((), ())), preferred_element_type=F32,
                              precision=lax.Precision.HIGHEST) for p in range(2)]
    top1, top2 = _top_values([s1, s2], PEER_TOPK)
    cand_scr[...] = jnp.full(cand_scr.shape, -jnp.inf, F32)
    for r, (a, b) in enumerate(_CAND_PAIRS):
        cand_scr[pl.ds(r, 1), :] = top1[a] + top2[b]
    best, = _top_values([cand_scr[...]], PEER_TOPK + 1)
    tau = 0.5 * (best[PEER_TOPK - 1] + best[PEER_TOPK])
    z = 1.0
    for k in range(1, PEER_TOPK):
        z = z + jnp.exp(best[k] - best[0])
    thr_ref[0] = tau - s1
    c_ref[0] = jnp.where(s1 >= top1[PEER_TOPK - 1], jnp.exp(s1 - top1[0]), 0.0) * (1.0 / z)
    s2_ref[0] = s2
    p2_ref[0] = jnp.where(s2 >= top2[PEER_TOPK - 1], jnp.exp(s2 - top2[0]), 0.0)


def _peer_selection(q, sub_keys, *, tq=512):
    t = q.shape[0]
    tq = _tile(t, tq)
    out = jax.ShapeDtypeStruct((PEER_HEADS, PEER_KEYS, t), F32)
    spec = pl.BlockSpec((1, PEER_KEYS, tq), lambda i, h: (h, 0, i))
    return pl.pallas_call(
        _peer_select_kernel,
        grid=(t // tq, PEER_HEADS),
        in_specs=[pl.BlockSpec((tq, PEER_QDIM), lambda i, h: (i, h)),
                  pl.BlockSpec((2, PEER_KEYS, PEER_HALF), lambda i, h: (0, 0, 0))],
        out_specs=[spec, spec, spec, spec],
        out_shape=[out, out, out, out],
        scratch_shapes=[pltpu.VMEM((_CAND_ROWS, tq), F32)],
        compiler_params=_cparams(("parallel", "parallel")),
        name="peer_select",
    )(q, sub_keys)


def _peer(x, g, shift, scale, gate, w_q, sub_keys, u_tab, v_tab, final_g, final_norm):
    q, h = _norm_mod_matmul(x, g, shift, scale, w_q.astype(BF16), emit_h=True)
    thr_t, c_t, s2_t, p2_t = _peer_selection(q, sub_keys)
    return _peer_experts(h, thr_t, c_t, s2_t, p2_t, u_tab.astype(BF16), v_tab.astype(BF16),
                         x, gate, final_g, final_norm=final_norm)


def _shift_rows(u, prev_row, next_row):
    n = u.shape[0]
    r = lax.broadcasted_iota(jnp.int32, u.shape, 0)
    up = jnp.where(r == 0, prev_row, pltpu.roll(u, 1, 0))
    dn = jnp.where(r == n - 1, next_row, pltpu.roll(u, n - 1, 0))
    return up, dn


def _halo_specs(tm, tc, n_row_blocks8, col):
    return [pl.BlockSpec((tm, tc), lambda i, j: (i, col + j)),
            pl.BlockSpec((8, tc), lambda i, j: (jnp.maximum(i * (tm // 8) - 1, 0), col + j)),
            pl.BlockSpec((8, tc), lambda i, j: (jnp.minimum((i + 1) * (tm // 8), n_row_blocks8 - 1),
                                                col + j))]


def _conv_gate_kernel(b_ref, c_ref, cp_ref, cn_ref, v_ref, vp_ref, vn_ref, w_ref, o_ref):
    i = pl.program_id(0)
    u = c_ref[...] * v_ref[...]
    prev_row = jnp.where(i == 0, 0.0, cp_ref[7:8, :] * vp_ref[7:8, :])
    next_row = jnp.where(i == pl.num_programs(0) - 1, 0.0, cn_ref[0:1, :] * vn_ref[0:1, :])
    up, dn = _shift_rows(u, prev_row, next_row)
    w = w_ref[...]
    o_ref[...] = (b_ref[...] * (up * w[0:1, :] + u * w[1:2, :] + dn * w[2:3, :])).astype(o_ref.dtype)


def _conv_gate(p, conv_w, *, tm=512, tc=512):
    l = p.shape[0]
    tm = _tile(l, tm, 8)
    nb = CONV_DIM // tc
    return pl.pallas_call(
        _conv_gate_kernel,
        grid=(l // tm, nb),
        in_specs=([pl.BlockSpec((tm, tc), lambda i, j: (i, j))]
                  + _halo_specs(tm, tc, l // 8, nb) + _halo_specs(tm, tc, l // 8, 2 * nb)
                  + [pl.BlockSpec((3, tc), lambda i, j: (0, j))]),
        out_specs=pl.BlockSpec((tm, tc), lambda i, j: (i, j)),
        out_shape=jax.ShapeDtypeStruct((l, CONV_DIM), BF16),
        compiler_params=_cparams(("parallel", "parallel")),
        name="conv_gate",
    )(*([p] * 7), conv_w)


def _rope_tables(n_rows):
    rr, cc = jnp.meshgrid(jnp.arange(n_rows), jnp.arange(GRID_W), indexing='ij')
    inv = ROPE_THETA ** (-jnp.arange(ROPE_AXIS // 2, dtype=F32) / (ROPE_AXIS // 2))
    ar = rr.reshape(-1, 1).astype(F32) * inv
    ac = cc.reshape(-1, 1).astype(F32) * inv
    ang = jnp.concatenate([ar, ar, ac, ac], axis=-1)
    pad = ((0, 0), (0, LANES - MLA_ROPE))
    return jnp.pad(jnp.cos(ang), pad), jnp.pad(jnp.sin(ang), pad)


def _rope_lanes(x, cos, sin):
    half = ROPE_AXIS // 2
    lane = lax.broadcasted_iota(jnp.int32, x.shape, 1)
    rot = jnp.where((lane & (ROPE_AXIS - 1)) < half,
                    -pltpu.roll(x, LANES - half, 1), pltpu.roll(x, half, 1))
    return x * cos + rot * sin


def _rms_rows(x, g):
    xf = x.astype(F32)
    return xf * lax.rsqrt(jnp.mean(xf * xf, axis=-1, keepdims=True) + NORM_EPS) * g


def _q_proj_kernel(x_ref, g_ref, w_ref, cos_ref, sin_ref, o_ref, h_ref):
    @pl.when(pl.program_id(1) == 0)
    def _():
        h_ref[...] = _rms_rows(x_ref[...], g_ref[...]).astype(BF16)

    y = jnp.dot(h_ref[...], w_ref[...], preferred_element_type=F32)
    y = jnp.concatenate([y[:, :MLA_NOPE], _rope_lanes(y[:, MLA_NOPE:], cos_ref[...], sin_ref[...])],
                        axis=-1)
    o_ref[0] = (y * (ATTN_SCALE * math.log2(math.e))).astype(o_ref.dtype)


def _q_proj(p, col, g, w_uq, cos, sin, *, tm=1024):
    l = p.shape[0]
    tm = _tile(l, tm, 8)
    w = jnp.pad(w_uq.reshape(MLA_Q_RANK, MLA_HEADS, MLA_QK),
                ((0, 0), (0, 0), (0, 2 * LANES - MLA_QK))).reshape(MLA_Q_RANK, -1).astype(BF16)
    return pl.pallas_call(
        _q_proj_kernel,
        grid=(l // tm, MLA_HEADS),
        in_specs=[pl.BlockSpec((tm, MLA_Q_RANK), lambda i, h: (i, col // MLA_Q_RANK)),
                  pl.BlockSpec((1, MLA_Q_RANK), lambda i, h: (0, 0)),
                  pl.BlockSpec((MLA_Q_RANK, 2 * LANES), lambda i, h: (0, h)),
                  pl.BlockSpec((tm, LANES), lambda i, h: (i, 0)),
                  pl.BlockSpec((tm, LANES), lambda i, h: (i, 0))],
        out_specs=pl.BlockSpec((1, tm, 2 * LANES), lambda i, h: (h, i, 0)),
        out_shape=jax.ShapeDtypeStruct((MLA_HEADS, l, 2 * LANES), BF16),
        scratch_shapes=[pltpu.VMEM((tm, MLA_Q_RANK), BF16)],
        compiler_params=_cparams(("parallel", "arbitrary")),
        name="q_proj",
    )(p, g.reshape(1, -1), w, cos, sin)


def _kv_proj_kernel(x_ref, kr_ref, g_ref, w_ref, cos_ref, sin_ref, k_ref, v_ref, h_ref, r_ref):
    @pl.when(pl.program_id(1) == 0)
    def _():
        h_ref[...] = _rms_rows(x_ref[...], g_ref[...]).astype(BF16)
        r_ref[...] = _rope_lanes(kr_ref[...], cos_ref[...], sin_ref[...]).astype(BF16)

    y = jnp.dot(h_ref[...], w_ref[...], preferred_element_type=F32)
    k_ref[0] = jnp.concatenate([y[:, :MLA_NOPE].astype(BF16), r_ref[...]], axis=-1)
    v_ref[0] = y[:, MLA_NOPE:].astype(BF16)


def _kv_proj(p, col_kv, col_kr, g, w_ukv, cos, sin, *, tm=1024):
    l = p.shape[0]
    tm = _tile(l, tm, 8)
    return pl.pallas_call(
        _kv_proj_kernel,
        grid=(l // tm, MLA_HEADS),
        in_specs=[pl.BlockSpec((tm, MLA_KV_RANK), lambda i, h: (i, col_kv // MLA_KV_RANK)),
                  pl.BlockSpec((tm, LANES), lambda i, h: (i, col_kr // LANES)),
                  pl.BlockSpec((1, MLA_KV_RANK), lambda i, h: (0, 0)),
                  pl.BlockSpec((MLA_KV_RANK, MLA_NOPE + MLA_V), lambda i, h: (0, h)),
                  pl.BlockSpec((tm, LANES), lambda i, h: (i, 0)),
                  pl.BlockSpec((tm, LANES), lambda i, h: (i, 0))],
        out_specs=[pl.BlockSpec((1, tm, 2 * LANES), lambda i, h: (h, i, 0)),
                   pl.BlockSpec((1, tm, MLA_V), lambda i, h: (h, i, 0))],
        out_shape=[jax.ShapeDtypeStruct((MLA_HEADS, l, 2 * LANES), BF16),
                   jax.ShapeDtypeStruct((MLA_HEADS, l, MLA_V), BF16)],
        scratch_shapes=[pltpu.VMEM((tm, MLA_KV_RANK), BF16), pltpu.VMEM((tm, LANES), BF16)],
        compiler_params=_cparams(("parallel", "arbitrary")),
        name="kv_proj",
    )(p, p, g.reshape(1, -1), w_ukv, cos, sin)


def _mixer_conv_mla(x, ctx, g, sh, sc, mc, gate, cos, sin, w_in, conv_w, q_norm_g, w_uq,
                    kv_norm_g, w_ukv, w_out):
    d = x.shape[1]
    c0 = 3 * CONV_DIM
    n_in = w_in.shape[1]
    p = _norm_mod_matmul(x, g, sh, sc, jnp.pad(w_in, ((0, 0), (0, -n_in % 512))).astype(BF16))
    y_conv = _conv_gate(p, conv_w)
    w_ukv_b = w_ukv.astype(BF16)
    q = _q_proj(p, c0, q_norm_g, w_uq, cos, sin)
    k, v = _kv_proj(p, c0 + MLA_Q_RANK, c0 + MLA_Q_RANK + MLA_KV_RANK, kv_norm_g, w_ukv_b, cos, sin)
    lc = ctx.shape[0]
    n_c = n_in - (c0 + MLA_Q_RANK)
    w_ctx = jnp.pad(w_in[:, c0 + MLA_Q_RANK:], ((0, 0), (0, -n_c % LANES))).astype(BF16)
    pc = _norm_mod_matmul(ctx, g, mc[:d], mc[d:], w_ctx)
    k_x, v_x = _kv_proj(pc, 0, MLA_KV_RANK, kv_norm_g, w_ukv_b,
                        jnp.ones((lc, LANES), F32), jnp.zeros((lc, LANES), F32))
    y_att = _attention(q, k_x, v_x, k, v)
    return _matmul2_residual(y_conv, y_att, w_out.astype(BF16), x, gate)


def _gmlp_kernel(u_ref, v_ref, g_ref, ws_ref, b_ref, o_ref, vn_ref):
    v = jax.nn.gelu(v_ref[...])
    ms = jnp.mean(v * v, axis=-1, keepdims=True)
    vn_ref[...] = (v * lax.rsqrt(ms + NORM_EPS) * g_ref[...]).astype(BF16)
    gw = GM_DIM // GM_GROUPS
    for ci in range(u_ref.shape[0] // GM_CHUNK):
        rows = slice(ci * GM_CHUNK, (ci + 1) * GM_CHUNK)
        for gi in range(GM_GROUPS):
            cols = slice(gi * gw, (gi + 1) * gw)
            s = jnp.dot(ws_ref[gi], vn_ref[rows, cols], preferred_element_type=F32) + b_ref[gi]
            o_ref[rows, cols] = (jax.nn.gelu(u_ref[rows, cols]) * s).astype(o_ref.dtype)


def _gmlp(p, gm_norm_g, gm_ws, gm_b, *, tm=512):
    l = p.shape[0]
    tm = _tile(l, tm, GM_CHUNK)
    return pl.pallas_call(
        _gmlp_kernel,
        grid=(l // tm,),
        in_specs=[pl.BlockSpec((tm, GM_DIM), lambda i: (i, 0)),
                  pl.BlockSpec((tm, GM_DIM), lambda i: (i, 1)),
                  pl.BlockSpec((1, GM_DIM), lambda i: (0, 0)),
                  pl.BlockSpec((GM_GROUPS, GM_CHUNK, GM_CHUNK), lambda i: (0, 0, 0)),
                  pl.BlockSpec((GM_GROUPS, GM_CHUNK, 1), lambda i: (0, 0, 0))],
        out_specs=pl.BlockSpec((tm, GM_DIM), lambda i: (i, 0)),
        out_shape=jax.ShapeDtypeStruct((l, GM_DIM), BF16),
        scratch_shapes=[pltpu.VMEM((tm, GM_DIM), BF16)],
        compiler_params=_cparams(("parallel",)),
        name="gmlp",
    )(p, p, gm_norm_g.reshape(1, GM_DIM), gm_ws.astype(BF16), gm_b.reshape(GM_GROUPS, GM_CHUNK, 1))


def _hyena_pre_kernel(*refs):
    ins, (w_ref, b_ref, x0_ref, g1_ref) = refs[:9], refs[9:]
    i = pl.program_id(0)
    first = i == 0
    last = i == pl.num_programs(0) - 1
    tc = x0_ref.shape[1]
    outs = []
    for part in range(3):
        main, prev, nxt = ins[3 * part:3 * part + 3]
        u = main[...]
        prev_row = jnp.where(first, 0.0, prev[7:8, :])
        next_row = jnp.where(last, 0.0, nxt[0:1, :])
        up, dn = _shift_rows(u, prev_row, next_row)
        w = w_ref[part]
        outs.append(up * w[0:1, :] + u * w[1:2, :] + dn * w[2:3, :] + b_ref[part])
    x0_ref[...] = outs[0]
    g1_ref[...] = outs[1] * outs[2]


def _hyena_pre(p, col0, conv_w, conv_b, *, tm=512, tc=512):
    l = p.shape[0]
    tm = _tile(l, tm, 8)
    nrb = l // 8
    specs = []
    for part in range(3):
        cb = (col0 + part * HY_DIM) // tc
        specs += [
            pl.BlockSpec((tm, tc), lambda i, j, cb=cb: (i, cb + j)),
            pl.BlockSpec((8, tc), lambda i, j, cb=cb: (jnp.maximum(i * (tm // 8) - 1, 0), cb + j)),
            pl.BlockSpec((8, tc), lambda i, j, cb=cb: (jnp.minimum((i + 1) * (tm // 8), nrb - 1), cb + j)),
        ]
    w = jnp.transpose(conv_w.reshape(3, 3, HY_DIM), (1, 0, 2))
    b = conv_b.reshape(3, 1, HY_DIM)
    out = jax.ShapeDtypeStruct((l, HY_DIM), F32)
    ospec = pl.BlockSpec((tm, tc), lambda i, j: (i, j))
    return pl.pallas_call(
        _hyena_pre_kernel,
        grid=(l // tm, HY_DIM // tc),
        in_specs=specs + [pl.BlockSpec((3, 3, tc), lambda i, j: (0, 0, j)),
                          pl.BlockSpec((3, 1, tc), lambda i, j: (0, 0, j))],
        out_specs=[ospec, ospec],
        out_shape=[out, out],
        compiler_params=_cparams(("parallel", "parallel")),
        name="hyena_pre",
    )(*([p] * 9), w, b)


def _hyena_filter_kernel(band_ref, phase_ref, w1_ref, b1_ref, f1_ref, w2_ref, b2_ref, f2_ref,
                         w3_ref, delta_ref, k_ref, sum_ref, *, seq):
    i = pl.program_id(0)
    tm = k_ref.shape[0]
    hi = lax.Precision.HIGHEST
    r = i * tm + lax.broadcasted_iota(jnp.int32, (tm, 1), 0)
    lag = jnp.where(r < seq, r, 2 * seq - r).astype(F32)
    t = lag * (1.0 / (seq - 1))
    w = lag * (2.0 * math.pi / seq)
    lane = lax.broadcasted_iota(jnp.int32, (tm, LANES), 1)
    z = jnp.where(lane == 0, t, jnp.where(lane <= 2 * HY_BANDS,
                                          jnp.cos(band_ref[...] * w + phase_ref[...]), 0.0))
    a = jnp.sin(f1_ref[...] * (jnp.dot(z, w1_ref[...], precision=hi,
                                       preferred_element_type=F32) + b1_ref[...]))
    a = jnp.sin(f2_ref[...] * (jnp.dot(a, w2_ref[...], precision=hi,
                                       preferred_element_type=F32) + b2_ref[...]))
    hf = jnp.dot(a, w3_ref[...], precision=hi, preferred_element_type=F32)
    k = jnp.where(r == seq, 0.0, hf * jnp.exp(-t * delta_ref[...]))
    k_ref[...] = k

    @pl.when(i == 0)
    def _():
        sum_ref[...] = jnp.zeros(sum_ref.shape, F32)

    sum_ref[...] += jnp.sum(jnp.abs(k), axis=0, keepdims=True)


def _hyena_filter(seq, w1, b1, f1, w2, b2, f2, w3, *, tm=512):
    import numpy as np
    n = 2 * seq
    tm = _tile(seq, tm, 8)
    bands = np.linspace(1e-4, HY_BANDS - 1, HY_BANDS)
    band = np.zeros((1, LANES), np.float32)
    phase = np.zeros((1, LANES), np.float32)
    band[0, 1:1 + HY_BANDS] = bands
    band[0, 1 + HY_BANDS:1 + 2 * HY_BANDS] = bands
    phase[0, 1 + HY_BANDS:1 + 2 * HY_BANDS] = 0.5 * math.pi
    deltas = np.abs(np.linspace(math.log(HY_TARGET) / HY_SLOW, math.log(HY_TARGET) / HY_FAST,
                                HY_DIM)).astype(np.float32).reshape(1, HY_DIM)
    emb, ffn = w1.shape
    pad2 = lambda a, rr, cc: jnp.pad(a, ((0, rr - a.shape[0]), (0, cc - a.shape[1])))
    row = lambda v: jnp.pad(v.reshape(1, -1), ((0, 0), (0, LANES - v.shape[0])))
    nhalf = seq // tm
    const = lambda shape: pl.BlockSpec(shape, lambda i: (0,) * len(shape))
    return pl.pallas_call(
        functools.partial(_hyena_filter_kernel, seq=seq),
        grid=(n // tm,),
        in_specs=[const((1, LANES)), const((1, LANES)),
                  const((LANES, LANES)), const((1, LANES)), const((1, LANES)),
                  const((LANES, LANES)), const((1, LANES)), const((1, LANES)),
                  pl.BlockSpec((LANES, HY_DIM), lambda i: (0, i // nhalf)),
                  const((1, HY_DIM))],
        out_specs=[pl.BlockSpec((tm, HY_DIM), lambda i: (i, 0)),
                   pl.BlockSpec((1, HY_DIM), lambda i: (0, 0))],
        out_shape=[jax.ShapeDtypeStruct((n, HY_DIM), F32),
                   jax.ShapeDtypeStruct((1, HY_DIM), F32)],
        compiler_params=_cparams(("arbitrary",)),
        name="hyena_filter",
    )(jnp.asarray(band), jnp.asarray(phase), pad2(w1, LANES, LANES), row(b1), row(f1),
      pad2(w2, LANES, LANES), row(b2), row(f2), pad2(w3, LANES, 2 * HY_DIM), jnp.asarray(deltas))


DFT_N2 = 256


def _dft_tables(n1):
    import numpy as np
    n2 = DFT_N2
    n = n1 * n2
    a1 = 2.0 * np.pi * np.outer(np.arange(n1), np.arange(n1)) / n1
    fwd1 = np.concatenate([np.cos(a1), -np.sin(a1)], axis=0)
    inv1 = np.concatenate([np.cos(a1), -np.sin(a1)], axis=1)[:n1 // 2]
    a2 = 2.0 * np.pi * np.outer(np.arange(n2), np.arange(n2)) / n2
    c2, s2 = np.cos(a2), np.sin(a2)
    fwd2 = np.block([[c2, s2], [-s2, c2]])
    inv2 = np.block([[c2, -s2], [s2, c2]])
    th = 2.0 * np.pi * np.outer(np.arange(n2), np.arange(n1)) / n
    bf = lambda a: jnp.asarray(a, dtype=BF16)
    f3 = lambda a: jnp.asarray(a[..., None], dtype=F32)
    return dict(fwd1=bf(fwd1), inv1=bf(inv1), fwd2=bf(fwd2), inv2=bf(inv2),
                twa_c=f3(np.cos(th)), twa_s=f3(np.sin(th)),
                twc_c=f3(np.cos(th).T / n), twc_s=f3(np.sin(th).T / n))


def _dft_a_kernel(x_ref, f_ref, c_ref, s_ref, re_ref, im_ref, *, n1):
    width = x_ref.shape[2]
    for sg in range(x_ref.shape[1]):
        cols = slice(sg * width, (sg + 1) * width)
        r = jnp.dot(f_ref[...], x_ref[:, sg, :].astype(BF16), preferred_element_type=F32)
        re, im = r[:n1], r[n1:]
        c, s = c_ref[sg], s_ref[sg]
        re_ref[:, cols] = (re * c + im * s).astype(re_ref.dtype)
        im_ref[:, cols] = (im * c - re * s).astype(im_ref.dtype)


def _dft_stage_a(x3, tab, n1, *, seg=8):
    k, n2, width = x3.shape
    tn = seg * width
    out = jax.ShapeDtypeStruct((n1, n2 * width), BF16)
    ospec = pl.BlockSpec((n1, tn), lambda j: (0, j))
    tspec = pl.BlockSpec((seg, n1, 1), lambda j: (j, 0, 0))
    return pl.pallas_call(
        functools.partial(_dft_a_kernel, n1=n1),
        grid=(n2 // seg,),
        in_specs=[pl.BlockSpec((k, seg, width), lambda j: (0, j, 0)),
                  pl.BlockSpec((2 * n1, k), lambda j: (0, 0)), tspec, tspec],
        out_specs=[ospec, ospec],
        out_shape=[out, out],
        compiler_params=_cparams(("parallel",)),
        name="dft_stage_a",
    )(x3, tab["fwd1"][:, :k], tab["twa_c"], tab["twa_s"])


def _dft_c_filter_kernel(re_ref, im_ref, f_ref, inv_ref, kre_ref, kim_ref):
    n2 = DFT_N2
    g = (jnp.dot(f_ref[:, :n2], re_ref[0], preferred_element_type=F32)
         + jnp.dot(f_ref[:, n2:], im_ref[0], preferred_element_type=F32))
    kre_ref[0] = g[:n2] * inv_ref[...]
    kim_ref[0] = g[n2:] * inv_ref[...]


def _dft_c_conv_kernel(re_ref, im_ref, kre_ref, kim_ref, f_ref, fi_ref, c_ref, s_ref,
                       ore_ref, oim_ref):
    n2 = DFT_N2
    g = (jnp.dot(f_ref[:, :n2], re_ref[0], preferred_element_type=F32)
         + jnp.dot(f_ref[:, n2:], im_ref[0], preferred_element_type=F32))
    gre, gim = g[:n2], g[n2:]
    kre, kim = kre_ref[0], kim_ref[0]
    yre = (gre * kre - gim * kim).astype(BF16)
    yim = (gre * kim + gim * kre).astype(BF16)
    b = (jnp.dot(fi_ref[:, :n2], yre, preferred_element_type=F32)
         + jnp.dot(fi_ref[:, n2:], yim, preferred_element_type=F32))
    bre, bim = b[:n2], b[n2:]
    c, s = c_ref[0], s_ref[0]
    ore_ref[0] = (bre * c - bim * s).astype(ore_ref.dtype)
    oim_ref[0] = (bre * s + bim * c).astype(oim_ref.dtype)


def _dft_stage_c_filter(re3, im3, tab, inv_norm, *, tc=512):
    n1, n2, c = re3.shape
    tc = _tile(c, tc)
    blk = pl.BlockSpec((1, n2, tc), lambda i, j: (i, 0, j))
    out = jax.ShapeDtypeStruct((n1, n2, c), F32)
    return pl.pallas_call(
        _dft_c_filter_kernel,
        grid=(n1, c // tc),
        in_specs=[blk, blk, pl.BlockSpec((2 * n2, 2 * n2), lambda i, j: (0, 0)),
                  pl.BlockSpec((1, tc), lambda i, j: (0, j))],
        out_specs=[blk, blk],
        out_shape=[out, out],
        compiler_params=_cparams(("parallel", "parallel")),
        name="dft_stage_c_filter",
    )(re3, im3, tab["fwd2"], inv_norm)


def _dft_stage_c_conv(re3, im3, kre, kim, tab, *, tc=512):
    n1, n2, c = re3.shape
    tc = _tile(c, tc)
    blk = pl.BlockSpec((1, n2, tc), lambda i, j: (i, 0, j))
    mat = pl.BlockSpec((2 * n2, 2 * n2), lambda i, j: (0, 0))
    tw = pl.BlockSpec((1, n2, 1), lambda i, j: (i, 0, 0))
    out = jax.ShapeDtypeStruct((n1, n2, c), BF16)
    return pl.pallas_call(
        _dft_c_conv_kernel,
        grid=(n1, c // tc),
        in_specs=[blk, blk, blk, blk, mat, mat, tw, tw],
        out_specs=[blk, blk],
        out_shape=[out, out],
        compiler_params=_cparams(("parallel", "parallel")),
        name="dft_stage_c_conv",
    )(re3, im3, kre, kim, tab["fwd2"], tab["inv2"], tab["twc_c"], tab["twc_s"])


def _dft_a_inv_kernel(re_ref, im_ref, f_ref, x0_ref, g1_ref, bias_ref, o_ref, *, n1):
    width = o_ref.shape[2]
    for sg in range(o_ref.shape[1]):
        cols = slice(sg * width, (sg + 1) * width)
        y = (jnp.dot(f_ref[:, :n1], re_ref[:, cols], preferred_element_type=F32)
             + jnp.dot(f_ref[:, n1:], im_ref[:, cols], preferred_element_type=F32))
        o_ref[:, sg, :] = x0_ref[:, sg, :] * (y + bias_ref[...] * g1_ref[:, sg, :])


def _dft_stage_a_inv(re2, im2, tab, x0_3, g1_3, bias_row, *, seg=8):
    n1 = re2.shape[0]
    half, n2, width = x0_3.shape
    tn = seg * width
    blk = pl.BlockSpec((n1, tn), lambda j: (0, j))
    hblk = pl.BlockSpec((half, seg, width), lambda j: (0, j, 0))
    return pl.pallas_call(
        functools.partial(_dft_a_inv_kernel, n1=n1),
        grid=(n2 // seg,),
        in_specs=[blk, blk, pl.BlockSpec((half, 2 * n1), lambda j: (0, 0)), hblk, hblk,
                  pl.BlockSpec((1, width), lambda j: (0, 0))],
        out_specs=hblk,
        out_shape=jax.ShapeDtypeStruct((half, n2, width), F32),
        compiler_params=_cparams(("parallel",)),
        name="dft_stage_a_inv",
    )(re2, im2, tab["inv1"], x0_3, g1_3, bias_row)


def _hyena(p, col0, conv_w, conv_b, w1, b1, f1, w2, b2, f2, w3, bias):
    l = p.shape[0]
    c = HY_DIM
    n1 = 2 * l // DFT_N2
    tab = _dft_tables(n1)
    x0, g1 = _hyena_pre(p, col0, conv_w, conv_b)
    filt, abs_sum = _hyena_filter(l, w1, b1, f1, w2, b2, f2, w3)
    inv_norm = 1.0 / (abs_sum + 1e-6)
    fre, fim = _dft_stage_a(filt.reshape(n1, DFT_N2, c), tab, n1)
    kre, kim = _dft_stage_c_filter(fre.reshape(n1, DFT_N2, c), fim.reshape(n1, DFT_N2, c), tab,
                                   inv_norm)
    g1_3 = g1.reshape(n1 // 2, DFT_N2, c)
    are, aim = _dft_stage_a(g1_3, tab, n1)
    bre, bim = _dft_stage_c_conv(are.reshape(n1, DFT_N2, c), aim.reshape(n1, DFT_N2, c),
                                 kre, kim, tab)
    y = _dft_stage_a_inv(bre.reshape(n1, DFT_N2 * c), bim.reshape(n1, DFT_N2 * c), tab,
                         x0.reshape(n1 // 2, DFT_N2, c), g1_3, bias.reshape(1, c))
    return y.reshape(l, c)


def _mm2res_kernel(y1_ref, y2_ref, w1_ref, w2_ref, x_ref, gate_ref, o_ref):
    acc = jnp.dot(y1_ref[...].astype(BF16), w1_ref[...], preferred_element_type=F32)
    acc += jnp.dot(y2_ref[...].astype(BF16), w2_ref[...], preferred_element_type=F32)
    o_ref[...] = x_ref[...] + gate_ref[...] * acc


def _matmul2_residual(y1, y2, w, x, gate, *, tm=1024, tn=1024):
    m, k1 = y1.shape
    k2 = y2.shape[1]
    n = w.shape[1]
    tm = _tile(m, tm, 8)
    tn = _tile(n, tn)
    kb = k1 // k2
    assert k1 == kb * k2
    return pl.pallas_call(
        _mm2res_kernel,
        grid=(m // tm, n // tn),
        in_specs=[pl.BlockSpec((tm, k1), lambda i, j: (i, 0)),
                  pl.BlockSpec((tm, k2), lambda i, j: (i, 0)),
                  pl.BlockSpec((k1, tn), lambda i, j: (0, j)),
                  pl.BlockSpec((k2, tn), lambda i, j: (kb, j)),
                  pl.BlockSpec((tm, tn), lambda i, j: (i, j)),
                  pl.BlockSpec((1, tn), lambda i, j: (0, j))],
        out_specs=pl.BlockSpec((tm, tn), lambda i, j: (i, j)),
        out_shape=jax.ShapeDtypeStruct((m, n), F32),
        compiler_params=_cparams(("parallel", "parallel")),
        name="matmul2_residual",
    )(y1, y2, w, w, x, gate.reshape(1, n).astype(F32))


def _mixer_gmlp_hyena(x, g, sh, sc, gate, w_in, gm_norm_g, gm_ws, gm_b, hy_conv_w, hy_conv_b,
                      hy_w1, hy_b1, hy_f1, hy_w2, hy_b2, hy_f2, hy_w3, hy_bias, w_out):
    p = _norm_mod_matmul(x, g, sh, sc, w_in.astype(BF16))
    y_gm = _gmlp(p, gm_norm_g, gm_ws, gm_b)
    y_hy = _hyena(p, 2 * GM_DIM, hy_conv_w, hy_conv_b, hy_w1, hy_b1, hy_f1, hy_w2, hy_b2,
                  hy_f2, hy_w3, hy_bias)
    return _matmul2_residual(y_gm, y_hy, w_out.astype(BF16), x, gate)


def kernel(x, c, ctx, c_ctx, ln1_g0, w_mod0, b_mod0, w_in0, conv_w0, q_norm_g0, w_uq0, kv_norm_g0, w_ukv0, w_out0, ln2_g0, peer_wq0, peer_keys0, peer_u0, peer_v0, ln1_g1, w_mod1, b_mod1, w_in1, gm_norm_g1, gm_ws1, gm_b1, hy_conv_w1, hy_conv_b1, hy_w1, hy_b1, hy_f1, hy_w2, hy_b2, hy_f2, hy_w3, hy_bias1, w_out1, ln2_g1, peer_wq1, peer_keys1, peer_u1, peer_v1, final_g):
    b, l, d = x.shape
    assert b == 1 and d == D_MODEL
    xt = x.reshape(l, d)
    cos, sin = _rope_tables(l // GRID_W)
    rows = jnp.zeros((8, d), F32).at[0].set(c[0]).at[1].set(c_ctx)

    mod0 = _mod_vectors(rows, w_mod0, b_mod0)
    sh_m, sc_m, g_m, sh_f, sc_f, g_f = [mod0[0, j * d:(j + 1) * d] for j in range(N_MOD)]
    mc = mod0[1, :2 * d]
    xt = _mixer_conv_mla(xt, ctx[0], ln1_g0, sh_m, sc_m, mc, g_m, cos, sin, w_in0, conv_w0,
                         q_norm_g0, w_uq0, kv_norm_g0, w_ukv0, w_out0)
    xt = _peer(xt, ln2_g0, sh_f, sc_f, g_f, peer_wq0, peer_keys0, peer_u0, peer_v0, final_g,
               final_norm=False)

    mod1 = _mod_vectors(rows, w_mod1, b_mod1)
    sh_m, sc_m, g_m, sh_f, sc_f, g_f = [mod1[0, j * d:(j + 1) * d] for j in range(N_MOD)]
    xt = _mixer_gmlp_hyena(xt, ln1_g1, sh_m, sc_m, g_m, w_in1, gm_norm_g1, gm_ws1, gm_b1,
                           hy_conv_w1, hy_conv_b1, hy_w1, hy_b1, hy_f1, hy_w2, hy_b2, hy_f2,
                           hy_w3, hy_bias1, w_out1)
    xt = _peer(xt, ln2_g1, sh_f, sc_f, g_f, peer_wq1, peer_keys1, peer_u1, peer_v1, final_g,
               final_norm=True)
    return xt.reshape(b, l, d)
```

```python
import functools
import math

import jax
import jax.numpy as jnp
from jax import lax
from jax.experimental import pallas as pl
from jax.experimental.pallas import tpu as pltpu

F32 = jnp.float32
BF16 = jnp.bfloat16

D_MODEL = 2048
GRID_W = 64
N_MOD = 6
NORM_EPS = 1e-6
CONV_DIM = D_MODEL // 2
MLA_HEADS = 8
MLA_NOPE = 128
MLA_ROPE = 64
MLA_V = 128
MLA_QK = MLA_NOPE + MLA_ROPE
MLA_Q_RANK = D_MODEL // 4
MLA_KV_RANK = D_MODEL // 8
ATTN_SCALE = MLA_QK ** -0.5
ROPE_AXIS = MLA_ROPE // 2
ROPE_THETA = 10000.0
GM_DIM = D_MODEL // 2
GM_GROUPS = 8
GM_CHUNK = 128
HY_DIM = D_MODEL // 2
HY_BANDS = 16
HY_TARGET = 1e-2
HY_FAST = 0.3
HY_SLOW = 1.5
PEER_KEYS = 128
PEER_HEADS = 8
PEER_TOPK = 16
PEER_QDIM = 256
PEER_HALF = PEER_QDIM // 2

LANES = 128
VMEM_LIMIT = 56 * 1024 * 1024

def _cparams(sem):
    return pltpu.CompilerParams(dimension_semantics=sem, vmem_limit_bytes=VMEM_LIMIT)


def _tile(n, target, mult=LANES):
    if n <= target:
        return n
    best = None
    for t in range(mult, target + 1, mult):
        if n % t == 0:
            best = t
    assert best is not None, (n, target, mult)
    return best


def _modvec_kernel(a_ref, w_ref, b_ref, o_ref):
    a = a_ref[...]
    a = a * jax.nn.sigmoid(a)
    o_ref[...] = jnp.dot(a, w_ref[...], preferred_element_type=F32,
                         precision=lax.Precision.HIGHEST) + b_ref[...]


def _mod_vectors(rows, w_mod, b_mod):
    d, n = w_mod.shape
    tn = _tile(n, 1024)
    return pl.pallas_call(
        _modvec_kernel,
        grid=(n // tn,),
        in_specs=[pl.BlockSpec((8, d), lambda j: (0, 0)),
                  pl.BlockSpec((d, tn), lambda j: (0, j)),
                  pl.BlockSpec((1, tn), lambda j: (0, j))],
        out_specs=pl.BlockSpec((8, tn), lambda j: (0, j)),
        out_shape=jax.ShapeDtypeStruct((8, n), F32),
        compiler_params=_cparams(("arbitrary",)),
        name="mod_vectors",
    )(rows, w_mod, b_mod.reshape(1, n))


def _nmm_kernel(x_ref, g_ref, sh_ref, sc_ref, w_ref, *rest, emit_h):
    if emit_h:
        o_ref, ho_ref, h_ref = rest
    else:
        o_ref, h_ref = rest

    @pl.when(pl.program_id(1) == 0)
    def _():
        xf = x_ref[...].astype(F32)
        ms = jnp.mean(xf * xf, axis=-1, keepdims=True)
        y = xf * lax.rsqrt(ms + NORM_EPS) * g_ref[...]
        h = (y * (1.0 + sc_ref[...]) + sh_ref[...]).astype(BF16)
        h_ref[...] = h
        if emit_h:
            ho_ref[...] = h

    o_ref[...] = jnp.dot(h_ref[...], w_ref[...], preferred_element_type=F32).astype(o_ref.dtype)


def _norm_mod_matmul(x, g, shift, scale, w, *, out_dtype=F32, tm=1024, tn=1024, emit_h=False):
    m, k = x.shape
    n = w.shape[1]
    tm = _tile(m, tm, 8)
    tn = _tile(n, tn)
    row = lambda v: v.reshape(1, k).astype(F32)
    out_shape = [jax.ShapeDtypeStruct((m, n), out_dtype)]
    out_specs = [pl.BlockSpec((tm, tn), lambda i, j: (i, j))]
    if emit_h:
        out_shape.append(jax.ShapeDtypeStruct((m, k), BF16))
        out_specs.append(pl.BlockSpec((tm, k), lambda i, j: (i, 0)))
    res = pl.pallas_call(
        functools.partial(_nmm_kernel, emit_h=emit_h),
        grid=(m // tm, n // tn),
        in_specs=[pl.BlockSpec((tm, k), lambda i, j: (i, 0)),
                  pl.BlockSpec((1, k), lambda i, j: (0, 0)),
                  pl.BlockSpec((1, k), lambda i, j: (0, 0)),
                  pl.BlockSpec((1, k), lambda i, j: (0, 0)),
                  pl.BlockSpec((k, tn), lambda i, j: (0, j))],
        out_specs=out_specs,
        out_shape=out_shape,
        scratch_shapes=[pltpu.VMEM((tm, k), BF16)],
        compiler_params=_cparams(("parallel", "arbitrary")),
        name="norm_mod_matmul",
    )(x, row(g), row(shift), row(scale), w)
    return res if emit_h else res[0]


def _flash_kernel(q_ref, kc_ref, vc_ref, k_ref, v_ref, o_ref, s_scr, p_scr, *, tk):
    q = q_ref[0]
    tq = q.shape[0]
    n_chunks = k_ref.shape[1] // tk

    def scores(kblk):
        return lax.dot_general(q, kblk, (((1,), (1,)), ((), ())), preferred_element_type=F32)

    def k_chunk(c):
        return k_ref[0, pl.ds(pl.multiple_of(c * tk, tk), tk), :]

    def v_chunk(c):
        return v_ref[0, pl.ds(pl.multiple_of(c * tk, tk), tk), :]

    def soft(s, m_prev, l_prev):
        m_new = jnp.maximum(m_prev, jnp.max(s, axis=-1, keepdims=True))
        alpha = jnp.exp2(m_prev - m_new)
        p = jnp.exp2(s - m_new)
        return m_new, alpha * l_prev + jnp.sum(p, axis=-1, keepdims=True), alpha, p.astype(BF16)

    def soft_slot(slot, m_prev, l_prev):
        m_new, l_new, alpha, p = soft(s_scr[slot], m_prev, l_prev)
        p_scr[slot] = p
        return m_new, l_new, alpha

    def pv(p, vblk, acc, alpha):
        return alpha * acc + jnp.dot(p, vblk, preferred_element_type=F32)

    m = jnp.full((tq, 1), -jnp.inf, F32)
    l = jnp.zeros((tq, 1), F32)
    acc = jnp.zeros((tq, v_ref.shape[2]), F32)
    s_scr[0] = scores(k_chunk(0))
    m, l, alpha, p = soft(scores(kc_ref[0]), m, l)
    acc = pv(p, vc_ref[0], acc, alpha)
    s_scr[1] = scores(k_chunk(1))
    m, l, alpha_e = soft_slot(0, m, l)

    def body(cc, carry):
        m, l, acc, alpha_e = carry
        c = 2 * cc + 1
        s_scr[0] = scores(k_chunk(c + 1))
        m, l, alpha_o = soft_slot(1, m, l)
        acc = pv(p_scr[0], v_chunk(c - 1), acc, alpha_e)
        s_scr[1] = scores(k_chunk(c + 2))
        m, l, alpha_e = soft_slot(0, m, l)
        acc = pv(p_scr[1], v_chunk(c), acc, alpha_o)
        return m, l, acc, alpha_e

    carry = (m, l, acc, alpha_e)
    for cc in range((n_chunks - 2) // 2):
        carry = body(cc, carry)
    m, l, acc, alpha_e = carry
    m, l, alpha_o = soft_slot(1, m, l)
    acc = pv(p_scr[0], v_chunk(n_chunks - 2), acc, alpha_e)
    acc = pv(p_scr[1], v_chunk(n_chunks - 1), acc, alpha_o)
    o_ref[...] = (acc / l).astype(o_ref.dtype)


def _attention(q, k_ctx, v_ctx, k, v, *, tq=512, tk=1024):
    h, l, dqk = q.shape
    lk, lc = k.shape[1], k_ctx.shape[1]
    dv = v.shape[2]
    tq = _tile(l, tq)
    tk = _tile(lk, min(tk, lk // 2))
    assert (lk // tk) % 2 == 0
    head = lambda n, dd: pl.BlockSpec((1, n, dd), lambda hh, i: (hh, 0, 0))
    return pl.pallas_call(
        functools.partial(_flash_kernel, tk=tk),
        grid=(h, l // tq),
        in_specs=[pl.BlockSpec((1, tq, dqk), lambda hh, i: (hh, i, 0)),
                  head(lc, dqk), head(lc, dv), head(lk, dqk), head(lk, dv)],
        out_specs=pl.BlockSpec((tq, dv), lambda hh, i: (i, hh)),
        out_shape=jax.ShapeDtypeStruct((l, h * dv), BF16),
        scratch_shapes=[pltpu.VMEM((2, tq, tk), F32), pltpu.VMEM((2, tq, tk), BF16)],
        compiler_params=_cparams(("parallel", "arbitrary")),
        name="mla_attention",
    )(q, k_ctx, v_ctx, k, v)


_GELU_A = -2.0 * math.sqrt(2.0 / math.pi) * math.log2(math.e)
_GELU_B = _GELU_A * 0.044715


def _gelu_tanh(a):
    return a * (1.0 / (1.0 + jnp.exp2(a * (_GELU_A + _GELU_B * (a * a)))))


def _peer_kernel(h_ref, thr_ref, c_ref, s2_ref, p2_ref, u_ref, v_ref, x_ref, gf_ref, fg_ref,
                 o_ref, *, final_norm):
    j = pl.program_id(1)
    te = u_ref.shape[0]
    n_e1 = te // PEER_KEYS

    @pl.when(j == 0)
    def _():
        o_ref[...] = jnp.zeros(o_ref.shape, F32)

    a = lax.dot_general(h_ref[...], u_ref[...], (((1,), (1,)), ((), ())),
                        preferred_element_type=F32)
    blocks = []
    for r in range(n_e1):
        w = None
        for hd in range(PEER_HEADS):
            thr_row = thr_ref[hd, r:r + 1, :]
            c_row = c_ref[hd, r:r + 1, :]
            term = jnp.where(s2_ref[hd] >= thr_row, p2_ref[hd], 0.0) * c_row
            w = term if w is None else w + term
        blocks.append(w)
    w_t = blocks[0] if n_e1 == 1 else jnp.concatenate(blocks, axis=0)
    wa = (w_t.T * _gelu_tanh(a)).astype(BF16)
    o_ref[...] += jnp.dot(wa, v_ref[...], preferred_element_type=F32)

    @pl.when(j == pl.num_programs(1) - 1)
    def _():
        y = x_ref[...] + gf_ref[...] * o_ref[...]
        if final_norm:
            ms = jnp.mean(y * y, axis=-1, keepdims=True)
            y = y * lax.rsqrt(ms + NORM_EPS) * fg_ref[...]
        o_ref[...] = y


def _peer_experts(h, thr_t, c_t, s2_t, p2_t, u_tab, v_tab, x, g_f, final_g, *, final_norm,
                  tq=512, te=1024):
    t, d = h.shape
    e = u_tab.shape[0]
    tq = _tile(t, tq)
    n_e1 = te // PEER_KEYS
    sel_spec = pl.BlockSpec((PEER_HEADS, PEER_KEYS, tq), lambda i, j: (0, 0, i))
    row_spec = pl.BlockSpec((PEER_HEADS, n_e1, tq), lambda i, j: (0, j, i))
    return pl.pallas_call(
        functools.partial(_peer_kernel, final_norm=final_norm),
        grid=(t // tq, e // te),
        in_specs=[pl.BlockSpec((tq, d), lambda i, j: (i, 0)),
                  row_spec, row_spec, sel_spec, sel_spec,
                  pl.BlockSpec((te, d), lambda i, j: (j, 0)),
                  pl.BlockSpec((te, d), lambda i, j: (j, 0)),
                  pl.BlockSpec((tq, d), lambda i, j: (i, 0)),
                  pl.BlockSpec((1, d), lambda i, j: (0, 0)),
                  pl.BlockSpec((1, d), lambda i, j: (0, 0))],
        out_specs=pl.BlockSpec((tq, d), lambda i, j: (i, 0)),
        out_shape=jax.ShapeDtypeStruct((t, d), F32),
        compiler_params=_cparams(("parallel", "arbitrary")),
        name="peer_experts",
    )(h, thr_t, c_t, s2_t, p2_t, u_tab, v_tab, x, g_f.reshape(1, d).astype(F32),
      final_g.reshape(1, d).astype(F32))


_CAND_PAIRS = [(a, b) for a in range(PEER_TOPK) for b in range(PEER_TOPK // (a + 1))]
_CAND_ROWS = -(-len(_CAND_PAIRS) // 8) * 8


def _top_values(works, n):
    vals = [[] for _ in works]
    for k in range(n):
        ms = [jnp.max(w, axis=0, keepdims=True) for w in works]
        for v, m in zip(vals, ms):
            v.append(m)
        if k + 1 < n:
            works = [jnp.where(w >= m, -jnp.inf, w) for w, m in zip(works, ms)]
    return vals


def _batcher_pairs(lo, hi):
    def merge(lo, hi, r):
        step = 2 * r
        if step < hi - lo:
            yield from merge(lo, hi, step)
            yield from merge(lo + r, hi, step)
            yield from ((i, i + r) for i in range(lo + r, hi - r, step))
        else:
            yield (lo, lo + r)
    if hi > lo:
        mid = lo + (hi - lo) // 2
        yield from _batcher_pairs(lo, mid)
        yield from _batcher_pairs(mid + 1, hi)
        yield from merge(lo, hi, 1)


_SORT_PAIRS = list(_batcher_pairs(0, PEER_TOPK - 1))
_BITONIC_PAIRS = [(i, i + d) for d in (8, 4, 2, 1) for i in range(PEER_TOPK) if (i // d) % 2 == 0]
SUBLANES = 8


def _compare_exchange(v, pairs):
    for a, b in pairs:
        v[a], v[b] = jnp.maximum(v[a], v[b]), jnp.minimum(v[a], v[b])


def _top16_sorted(s):
    v = [s[SUBLANES * i:SUBLANES * (i + 1), :] for i in range(PEER_TOPK)]
    _compare_exchange(v, _SORT_PAIRS)
    shift = SUBLANES // 2
    while shift:
        v = [jnp.maximum(v[i], pltpu.roll(v[PEER_TOPK - 1 - i], shift, 0))
             for i in range(PEER_TOPK)]
        _compare_exchange(v, _BITONIC_PAIRS)
        shift //= 2
    return [x[0:1, :] for x in v]


def _peer_select_kernel(q_ref, keys_ref, thr_ref, c_ref, s2_ref, p2_ref, cand_scr):
    s1, s2 = [lax.dot_general(keys_ref[p], q_ref[:, p * PEER_HALF:(p + 1) * PEER_HALF],
                              (((1,), (1,)), ((), ())), preferred_element_type=F32,
                              precision=lax.Precision.HIGHEST) for p in range(2)]
    top1, top2 = _top16_sorted(s1), _top16_sorted(s2)
    cand_scr[...] = jnp.full(cand_scr.shape, -jnp.inf, F32)
    for r, (a, b) in enumerate(_CAND_PAIRS):
        cand_scr[pl.ds(r, 1), :] = top1[a] + top2[b]
    best, = _top_values([cand_scr[...]], PEER_TOPK + 1)
    tau = 0.5 * (best[PEER_TOPK - 1] + best[PEER_TOPK])
    z = 1.0
    for k in range(1, PEER_TOPK):
        z = z + jnp.exp(best[k] - best[0])
    thr_ref[0] = tau - s1
    c_ref[0] = jnp.where(s1 >= top1[PEER_TOPK - 1], jnp.exp(s1 - top1[0]), 0.0) * (1.0 / z)
    s2_ref[0] = s2
    p2_ref[0] = jnp.where(s2 >= top2[PEER_TOPK - 1], jnp.exp(s2 - top2[0]), 0.0)


def _peer_selection(q, sub_keys, *, tq=512):
    t = q.shape[0]
    tq = _tile(t, tq)
    out = jax.ShapeDtypeStruct((PEER_HEADS, PEER_KEYS, t), F32)
    spec = pl.BlockSpec((1, PEER_KEYS, tq), lambda i, h: (h, 0, i))
    return pl.pallas_call(
        _peer_select_kernel,
        grid=(t // tq, PEER_HEADS),
        in_specs=[pl.BlockSpec((tq, PEER_QDIM), lambda i, h: (i, h)),
                  pl.BlockSpec((2, PEER_KEYS, PEER_HALF), lambda i, h: (0, 0, 0))],
        out_specs=[spec, spec, spec, spec],
        out_shape=[out, out, out, out],
        scratch_shapes=[pltpu.VMEM((_CAND_ROWS, tq), F32)],
        compiler_params=_cparams(("parallel", "parallel")),
        name="peer_select",
    )(q, sub_keys)


def _peer(x, g, shift, scale, gate, w_q, sub_keys, u_tab, v_tab, final_g, final_norm):
    q, h = _norm_mod_matmul(x, g, shift, scale, w_q.astype(BF16), emit_h=True)
    thr_t, c_t, s2_t, p2_t = _peer_selection(q, sub_keys)
    return _peer_experts(h, thr_t, c_t, s2_t, p2_t, u_tab.astype(BF16), v_tab.astype(BF16),
                         x, gate, final_g, final_norm=final_norm)


def _shift_rows(u, prev_row, next_row):
    n = u.shape[0]
    r = lax.broadcasted_iota(jnp.int32, u.shape, 0)
    up = jnp.where(r == 0, prev_row, pltpu.roll(u, 1, 0))
    dn = jnp.where(r == n - 1, next_row, pltpu.roll(u, n - 1, 0))
    return up, dn


def _halo_specs(tm, tc, n_row_blocks8, col):
    return [pl.BlockSpec((tm, tc), lambda i, j: (i, col + j)),
            pl.BlockSpec((8, tc), lambda i, j: (jnp.maximum(i * (tm // 8) - 1, 0), col + j)),
            pl.BlockSpec((8, tc), lambda i, j: (jnp.minimum((i + 1) * (tm // 8), n_row_blocks8 - 1),
                                                col + j))]


def _conv_gate_kernel(b_ref, c_ref, cp_ref, cn_ref, v_ref, vp_ref, vn_ref, w_ref, o_ref):
    i = pl.program_id(0)
    u = c_ref[...] * v_ref[...]
    prev_row = jnp.where(i == 0, 0.0, cp_ref[7:8, :] * vp_ref[7:8, :])
    next_row = jnp.where(i == pl.num_programs(0) - 1, 0.0, cn_ref[0:1, :] * vn_ref[0:1, :])
    up, dn = _shift_rows(u, prev_row, next_row)
    w = w_ref[...]
    o_ref[...] = (b_ref[...] * (up * w[0:1, :] + u * w[1:2, :] + dn * w[2:3, :])).astype(o_ref.dtype)


def _conv_gate(p, conv_w, *, tm=512, tc=512):
    l = p.shape[0]
    tm = _tile(l, tm, 8)
    nb = CONV_DIM // tc
    return pl.pallas_call(
        _conv_gate_kernel,
        grid=(l // tm, nb),
        in_specs=([pl.BlockSpec((tm, tc), lambda i, j: (i, j))]
                  + _halo_specs(tm, tc, l // 8, nb) + _halo_specs(tm, tc, l // 8, 2 * nb)
                  + [pl.BlockSpec((3, tc), lambda i, j: (0, j))]),
        out_specs=pl.BlockSpec((tm, tc), lambda i, j: (i, j)),
        out_shape=jax.ShapeDtypeStruct((l, CONV_DIM), BF16),
        compiler_params=_cparams(("parallel", "parallel")),
        name="conv_gate",
    )(*([p] * 7), conv_w)


def _rope_tables(n_rows):
    rr, cc = jnp.meshgrid(jnp.arange(n_rows), jnp.arange(GRID_W), indexing='ij')
    inv = ROPE_THETA ** (-jnp.arange(ROPE_AXIS // 2, dtype=F32) / (ROPE_AXIS // 2))
    ar = rr.reshape(-1, 1).astype(F32) * inv
    ac = cc.reshape(-1, 1).astype(F32) * inv
    ang = jnp.concatenate([ar, ar, ac, ac], axis=-1)
    pad = ((0, 0), (0, LANES - MLA_ROPE))
    return jnp.pad(jnp.cos(ang), pad), jnp.pad(jnp.sin(ang), pad)


def _rope_lanes(x, cos, sin):
    half = ROPE_AXIS // 2
    lane = lax.broadcasted_iota(jnp.int32, x.shape, 1)
    rot = jnp.where((lane & (ROPE_AXIS - 1)) < half,
                    -pltpu.roll(x, LANES - half, 1), pltpu.roll(x, half, 1))
    return x * cos + rot * sin


def _rms_rows(x, g):
    xf = x.astype(F32)
    return xf * lax.rsqrt(jnp.mean(xf * xf, axis=-1, keepdims=True) + NORM_EPS) * g


def _q_proj_kernel(x_ref, g_ref, w_ref, cos_ref, sin_ref, o_ref, h_ref):
    @pl.when(pl.program_id(1) == 0)
    def _():
        h_ref[...] = _rms_rows(x_ref[...], g_ref[...]).astype(BF16)

    y = jnp.dot(h_ref[...], w_ref[...], preferred_element_type=F32)
    y = jnp.concatenate([y[:, :MLA_NOPE], _rope_lanes(y[:, MLA_NOPE:], cos_ref[...], sin_ref[...])],
                        axis=-1)
    o_ref[0] = (y * (ATTN_SCALE * math.log2(math.e))).astype(o_ref.dtype)


def _q_proj(p, col, g, w_uq, cos, sin, *, tm=1024):
    l = p.shape[0]
    tm = _tile(l, tm, 8)
    w = jnp.pad(w_uq.reshape(MLA_Q_RANK, MLA_HEADS, MLA_QK),
                ((0, 0), (0, 0), (0, 2 * LANES - MLA_QK))).reshape(MLA_Q_RANK, -1).astype(BF16)
    return pl.pallas_call(
        _q_proj_kernel,
        grid=(l // tm, MLA_HEADS),
        in_specs=[pl.BlockSpec((tm, MLA_Q_RANK), lambda i, h: (i, col // MLA_Q_RANK)),
                  pl.BlockSpec((1, MLA_Q_RANK), lambda i, h: (0, 0)),
                  pl.BlockSpec((MLA_Q_RANK, 2 * LANES), lambda i, h: (0, h)),
                  pl.BlockSpec((tm, LANES), lambda i, h: (i, 0)),
                  pl.BlockSpec((tm, LANES), lambda i, h: (i, 0))],
        out_specs=pl.BlockSpec((1, tm, 2 * LANES), lambda i, h: (h, i, 0)),
        out_shape=jax.ShapeDtypeStruct((MLA_HEADS, l, 2 * LANES), BF16),
        scratch_shapes=[pltpu.VMEM((tm, MLA_Q_RANK), BF16)],
        compiler_params=_cparams(("parallel", "arbitrary")),
        name="q_proj",
    )(p, g.reshape(1, -1), w, cos, sin)


def _kv_proj_kernel(x_ref, kr_ref, g_ref, w_ref, cos_ref, sin_ref, k_ref, v_ref, h_ref, r_ref):
    @pl.when(pl.program_id(1) == 0)
    def _():
        h_ref[...] = _rms_rows(x_ref[...], g_ref[...]).astype(BF16)
        r_ref[...] = _rope_lanes(kr_ref[...], cos_ref[...], sin_ref[...]).astype(BF16)

    y = jnp.dot(h_ref[...], w_ref[...], preferred_element_type=F32)
    k_ref[0] = jnp.concatenate([y[:, :MLA_NOPE].astype(BF16), r_ref[...]], axis=-1)
    v_ref[0] = y[:, MLA_NOPE:].astype(BF16)


def _kv_proj(p, col_kv, col_kr, g, w_ukv, cos, sin, *, tm=1024):
    l = p.shape[0]
    tm = _tile(l, tm, 8)
    return pl.pallas_call(
        _kv_proj_kernel,
        grid=(l // tm, MLA_HEADS),
        in_specs=[pl.BlockSpec((tm, MLA_KV_RANK), lambda i, h: (i, col_kv // MLA_KV_RANK)),
                  pl.BlockSpec((tm, LANES), lambda i, h: (i, col_kr // LANES)),
                  pl.BlockSpec((1, MLA_KV_RANK), lambda i, h: (0, 0)),
                  pl.BlockSpec((MLA_KV_RANK, MLA_NOPE + MLA_V), lambda i, h: (0, h)),
                  pl.BlockSpec((tm, LANES), lambda i, h: (i, 0)),
                  pl.BlockSpec((tm, LANES), lambda i, h: (i, 0))],
        out_specs=[pl.BlockSpec((1, tm, 2 * LANES), lambda i, h: (h, i, 0)),
                   pl.BlockSpec((1, tm, MLA_V), lambda i, h: (h, i, 0))],
        out_shape=[jax.ShapeDtypeStruct((MLA_HEADS, l, 2 * LANES), BF16),
                   jax.ShapeDtypeStruct((MLA_HEADS, l, MLA_V), BF16)],
        scratch_shapes=[pltpu.VMEM((tm, MLA_KV_RANK), BF16), pltpu.VMEM((tm, LANES), BF16)],
        compiler_params=_cparams(("parallel", "arbitrary")),
        name="kv_proj",
    )(p, p, g.reshape(1, -1), w_ukv, cos, sin)


def _mixer_conv_mla(x, ctx, g, sh, sc, mc, gate, cos, sin, w_in, conv_w, q_norm_g, w_uq,
                    kv_norm_g, w_ukv, w_out):
    d = x.shape[1]
    c0 = 3 * CONV_DIM
    n_in = w_in.shape[1]
    p = _norm_mod_matmul(x, g, sh, sc, jnp.pad(w_in, ((0, 0), (0, -n_in % 512))).astype(BF16))
    y_conv = _conv_gate(p, conv_w)
    w_ukv_b = w_ukv.astype(BF16)
    q = _q_proj(p, c0, q_norm_g, w_uq, cos, sin)
    k, v = _kv_proj(p, c0 + MLA_Q_RANK, c0 + MLA_Q_RANK + MLA_KV_RANK, kv_norm_g, w_ukv_b, cos, sin)
    lc = ctx.shape[0]
    n_c = n_in - (c0 + MLA_Q_RANK)
    w_ctx = jnp.pad(w_in[:, c0 + MLA_Q_RANK:], ((0, 0), (0, -n_c % LANES))).astype(BF16)
    pc = _norm_mod_matmul(ctx, g, mc[:d], mc[d:], w_ctx)
    k_x, v_x = _kv_proj(pc, 0, MLA_KV_RANK, kv_norm_g, w_ukv_b,
                        jnp.ones((lc, LANES), F32), jnp.zeros((lc, LANES), F32))
    y_att = _attention(q, k_x, v_x, k, v)
    return _matmul2_residual(y_conv, y_att, w_out.astype(BF16), x, gate)


def _gmlp_kernel(u_ref, v_ref, g_ref, ws_ref, b_ref, o_ref, vn_ref):
    v = jax.nn.gelu(v_ref[...])
    ms = jnp.mean(v * v, axis=-1, keepdims=True)
    vn_ref[...] = (v * lax.rsqrt(ms + NORM_EPS) * g_ref[...]).astype(BF16)
    gw = GM_DIM // GM_GROUPS
    for ci in range(u_ref.shape[0] // GM_CHUNK):
        rows = slice(ci * GM_CHUNK, (ci + 1) * GM_CHUNK)
        for gi in range(GM_GROUPS):
            cols = slice(gi * gw, (gi + 1) * gw)
            s = jnp.dot(ws_ref[gi], vn_ref[rows, cols], preferred_element_type=F32) + b_ref[gi]
            o_ref[rows, cols] = (jax.nn.gelu(u_ref[rows, cols]) * s).astype(o_ref.dtype)


def _gmlp(p, gm_norm_g, gm_ws, gm_b, *, tm=512):
    l = p.shape[0]
    tm = _tile(l, tm, GM_CHUNK)
    return pl.pallas_call(
        _gmlp_kernel,
        grid=(l // tm,),
        in_specs=[pl.BlockSpec((tm, GM_DIM), lambda i: (i, 0)),
                  pl.BlockSpec((tm, GM_DIM), lambda i: (i, 1)),
                  pl.BlockSpec((1, GM_DIM), lambda i: (0, 0)),
                  pl.BlockSpec((GM_GROUPS, GM_CHUNK, GM_CHUNK), lambda i: (0, 0, 0)),
                  pl.BlockSpec((GM_GROUPS, GM_CHUNK, 1), lambda i: (0, 0, 0))],
        out_specs=pl.BlockSpec((tm, GM_DIM), lambda i: (i, 0)),
        out_shape=jax.ShapeDtypeStruct((l, GM_DIM), BF16),
        scratch_shapes=[pltpu.VMEM((tm, GM_DIM), BF16)],
        compiler_params=_cparams(("parallel",)),
        name="gmlp",
    )(p, p, gm_norm_g.reshape(1, GM_DIM), gm_ws.astype(BF16), gm_b.reshape(GM_GROUPS, GM_CHUNK, 1))


def _hyena_pre_kernel(*refs):
    ins, (w_ref, b_ref, x0_ref, g1_ref) = refs[:9], refs[9:]
    i = pl.program_id(0)
    first = i == 0
    last = i == pl.num_programs(0) - 1
    tc = x0_ref.shape[1]
    outs = []
    for part in range(3):
        main, prev, nxt = ins[3 * part:3 * part + 3]
        u = main[...]
        prev_row = jnp.where(first, 0.0, prev[7:8, :])
        next_row = jnp.where(last, 0.0, nxt[0:1, :])
        up, dn = _shift_rows(u, prev_row, next_row)
        w = w_ref[part]
        outs.append(up * w[0:1, :] + u * w[1:2, :] + dn * w[2:3, :] + b_ref[part])
    x0_ref[...] = outs[0]
    g1_ref[...] = outs[1] * outs[2]


def _hyena_pre(p, col0, conv_w, conv_b, *, tm=512, tc=512):
    l = p.shape[0]
    tm = _tile(l, tm, 8)
    nrb = l // 8
    specs = []
    for part in range(3):
        cb = (col0 + part * HY_DIM) // tc
        specs += [
            pl.BlockSpec((tm, tc), lambda i, j, cb=cb: (i, cb + j)),
            pl.BlockSpec((8, tc), lambda i, j, cb=cb: (jnp.maximum(i * (tm // 8) - 1, 0), cb + j)),
            pl.BlockSpec((8, tc), lambda i, j, cb=cb: (jnp.minimum((i + 1) * (tm // 8), nrb - 1), cb + j)),
        ]
    w = jnp.transpose(conv_w.reshape(3, 3, HY_DIM), (1, 0, 2))
    b = conv_b.reshape(3, 1, HY_DIM)
    out = jax.ShapeDtypeStruct((l, HY_DIM), F32)
    ospec = pl.BlockSpec((tm, tc), lambda i, j: (i, j))
    return pl.pallas_call(
        _hyena_pre_kernel,
        grid=(l // tm, HY_DIM // tc),
        in_specs=specs + [pl.BlockSpec((3, 3, tc), lambda i, j: (0, 0, j)),
                          pl.BlockSpec((3, 1, tc), lambda i, j: (0, 0, j))],
        out_specs=[ospec, ospec],
        out_shape=[out, out],
        compiler_params=_cparams(("parallel", "parallel")),
        name="hyena_pre",
    )(*([p] * 9), w, b)


def _hyena_filter_kernel(band_ref, phase_ref, w1_ref, b1_ref, f1_ref, w2_ref, b2_ref, f2_ref,
                         w3_ref, delta_ref, k_ref, sum_ref, *, seq):
    i = pl.program_id(0)
    tm = k_ref.shape[0]
    hi = lax.Precision.HIGHEST
    r = i * tm + lax.broadcasted_iota(jnp.int32, (tm, 1), 0)
    lag = jnp.where(r < seq, r, 2 * seq - r).astype(F32)
    t = lag * (1.0 / (seq - 1))
    w = lag * (2.0 * math.pi / seq)
    lane = lax.broadcasted_iota(jnp.int32, (tm, LANES), 1)
    z = jnp.where(lane == 0, t, jnp.where(lane <= 2 * HY_BANDS,
                                          jnp.cos(band_ref[...] * w + phase_ref[...]), 0.0))
    a = jnp.sin(f1_ref[...] * (jnp.dot(z, w1_ref[...], precision=hi,
                                       preferred_element_type=F32) + b1_ref[...]))
    a = jnp.sin(f2_ref[...] * (jnp.dot(a, w2_ref[...], precision=hi,
                                       preferred_element_type=F32) + b2_ref[...]))
    hf = jnp.dot(a, w3_ref[...], precision=hi, preferred_element_type=F32)
    k = jnp.where(r == seq, 0.0, hf * jnp.exp(-t * delta_ref[...]))
    k_ref[...] = k

    @pl.when(i == 0)
    def _():
        sum_ref[...] = jnp.zeros(sum_ref.shape, F32)

    sum_ref[...] += jnp.sum(jnp.abs(k), axis=0, keepdims=True)


def _hyena_filter(seq, w1, b1, f1, w2, b2, f2, w3, *, tm=512):
    import numpy as np
    n = 2 * seq
    tm = _tile(seq, tm, 8)
    bands = np.linspace(1e-4, HY_BANDS - 1, HY_BANDS)
    band = np.zeros((1, LANES), np.float32)
    phase = np.zeros((1, LANES), np.float32)
    band[0, 1:1 + HY_BANDS] = bands
    band[0, 1 + HY_BANDS:1 + 2 * HY_BANDS] = bands
    phase[0, 1 + HY_BANDS:1 + 2 * HY_BANDS] = 0.5 * math.pi
    deltas = np.abs(np.linspace(math.log(HY_TARGET) / HY_SLOW, math.log(HY_TARGET) / HY_FAST,
                                HY_DIM)).astype(np.float32).reshape(1, HY_DIM)
    emb, ffn = w1.shape
    pad2 = lambda a, rr, cc: jnp.pad(a, ((0, rr - a.shape[0]), (0, cc - a.shape[1])))
    row = lambda v: jnp.pad(v.reshape(1, -1), ((0, 0), (0, LANES - v.shape[0])))
    nhalf = seq // tm
    const = lambda shape: pl.BlockSpec(shape, lambda i: (0,) * len(shape))
    return pl.pallas_call(
        functools.partial(_hyena_filter_kernel, seq=seq),
        grid=(n // tm,),
        in_specs=[const((1, LANES)), const((1, LANES)),
                  const((LANES, LANES)), const((1, LANES)), const((1, LANES)),
                  const((LANES, LANES)), const((1, LANES)), const((1, LANES)),
                  pl.BlockSpec((LANES, HY_DIM), lambda i: (0, i // nhalf)),
                  const((1, HY_DIM))],
        out_specs=[pl.BlockSpec((tm, HY_DIM), lambda i: (i, 0)),
                   pl.BlockSpec((1, HY_DIM), lambda i: (0, 0))],
        out_shape=[jax.ShapeDtypeStruct((n, HY_DIM), F32),
                   jax.ShapeDtypeStruct((1, HY_DIM), F32)],
        compiler_params=_cparams(("arbitrary",)),
        name="hyena_filter",
    )(jnp.asarray(band), jnp.asarray(phase), pad2(w1, LANES, LANES), row(b1), row(f1),
      pad2(w2, LANES, LANES), row(b2), row(f2), pad2(w3, LANES, 2 * HY_DIM), jnp.asarray(deltas))


DFT_N2 = 256


def _dft_tables(n1):
    import numpy as np
    n2 = DFT_N2
    n = n1 * n2
    a1 = 2.0 * np.pi * np.outer(np.arange(n1), np.arange(n1)) / n1
    fwd1 = np.concatenate([np.cos(a1), -np.sin(a1)], axis=0)
    inv1 = np.concatenate([np.cos(a1), -np.sin(a1)], axis=1)[:n1 // 2]
    a2 = 2.0 * np.pi * np.outer(np.arange(n2), np.arange(n2)) / n2
    c2, s2 = np.cos(a2), np.sin(a2)
    fwd2 = np.block([[c2, s2], [-s2, c2]])
    inv2 = np.block([[c2, -s2], [s2, c2]])
    th = 2.0 * np.pi * np.outer(np.arange(n2), np.arange(n1)) / n
    bf = lambda a: jnp.asarray(a, dtype=BF16)
    f3 = lambda a: jnp.asarray(a[..., None], dtype=F32)
    return dict(fwd1=bf(fwd1), inv1=bf(inv1), fwd2=bf(fwd2), inv2=bf(inv2),
                twa_c=f3(np.cos(th)), twa_s=f3(np.sin(th)),
                twc_c=f3(np.cos(th).T / n), twc_s=f3(np.sin(th).T / n))


def _dft_a_kernel(x_ref, f_ref, c_ref, s_ref, re_ref, im_ref, *, n1):
    width = x_ref.shape[2]
    for sg in range(x_ref.shape[1]):
        cols = slice(sg * width, (sg + 1) * width)
        r = jnp.dot(f_ref[...], x_ref[:, sg, :].astype(BF16), preferred_element_type=F32)
        re, im = r[:n1], r[n1:]
        c, s = c_ref[sg], s_ref[sg]
        re_ref[:, cols] = (re * c + im * s).astype(re_ref.dtype)
        im_ref[:, cols] = (im * c - re * s).astype(im_ref.dtype)


def _dft_stage_a(x3, tab, n1, *, seg=8):
    k, n2, width = x3.shape
    tn = seg * width
    out = jax.ShapeDtypeStruct((n1, n2 * width), BF16)
    ospec = pl.BlockSpec((n1, tn), lambda j: (0, j))
    tspec = pl.BlockSpec((seg, n1, 1), lambda j: (j, 0, 0))
    return pl.pallas_call(
        functools.partial(_dft_a_kernel, n1=n1),
        grid=(n2 // seg,),
        in_specs=[pl.BlockSpec((k, seg, width), lambda j: (0, j, 0)),
                  pl.BlockSpec((2 * n1, k), lambda j: (0, 0)), tspec, tspec],
        out_specs=[ospec, ospec],
        out_shape=[out, out],
        compiler_params=_cparams(("parallel",)),
        name="dft_stage_a",
    )(x3, tab["fwd1"][:, :k], tab["twa_c"], tab["twa_s"])


def _dft_c_filter_kernel(re_ref, im_ref, f_ref, inv_ref, kre_ref, kim_ref):
    n2 = DFT_N2
    g = (jnp.dot(f_ref[:, :n2], re_ref[0], preferred_element_type=F32)
         + jnp.dot(f_ref[:, n2:], im_ref[0], preferred_element_type=F32))
    kre_ref[0] = (g[:n2] * inv_ref[...]).astype(kre_ref.dtype)
    kim_ref[0] = (g[n2:] * inv_ref[...]).astype(kim_ref.dtype)


def _dft_c_conv_kernel(re_ref, im_ref, kre_ref, kim_ref, f_ref, fi_ref, c_ref, s_ref,
                       ore_ref, oim_ref):
    n2 = DFT_N2
    g = (jnp.dot(f_ref[:, :n2], re_ref[0], preferred_element_type=F32)
         + jnp.dot(f_ref[:, n2:], im_ref[0], preferred_element_type=F32))
    gre, gim = g[:n2], g[n2:]
    kre, kim = kre_ref[0].astype(F32), kim_ref[0].astype(F32)
    yre = (gre * kre - gim * kim).astype(BF16)
    yim = (gre * kim + gim * kre).astype(BF16)
    b = (jnp.dot(fi_ref[:, :n2], yre, preferred_element_type=F32)
         + jnp.dot(fi_ref[:, n2:], yim, preferred_element_type=F32))
    bre, bim = b[:n2], b[n2:]
    c, s = c_ref[0], s_ref[0]
    ore_ref[0] = (bre * c - bim * s).astype(ore_ref.dtype)
    oim_ref[0] = (bre * s + bim * c).astype(oim_ref.dtype)


def _dft_stage_c_filter(re3, im3, tab, inv_norm, *, tc=1024):
    n1, n2, c = re3.shape
    tc = _tile(c, tc)
    blk = pl.BlockSpec((1, n2, tc), lambda i, j: (i, 0, j))
    out = jax.ShapeDtypeStruct((n1, n2, c), BF16)
    return pl.pallas_call(
        _dft_c_filter_kernel,
        grid=(n1, c // tc),
        in_specs=[blk, blk, pl.BlockSpec((2 * n2, 2 * n2), lambda i, j: (0, 0)),
                  pl.BlockSpec((1, tc), lambda i, j: (0, j))],
        out_specs=[blk, blk],
        out_shape=[out, out],
        compiler_params=_cparams(("parallel", "parallel")),
        name="dft_stage_c_filter",
    )(re3, im3, tab["fwd2"], inv_norm)


def _dft_stage_c_conv(re3, im3, kre, kim, tab, *, tc=1024):
    n1, n2, c = re3.shape
    tc = _tile(c, tc)
    blk = pl.BlockSpec((1, n2, tc), lambda i, j: (i, 0, j))
    mat = pl.BlockSpec((2 * n2, 2 * n2), lambda i, j: (0, 0))
    tw = pl.BlockSpec((1, n2, 1), lambda i, j: (i, 0, 0))
    out = jax.ShapeDtypeStruct((n1, n2, c), BF16)
    return pl.pallas_call(
        _dft_c_conv_kernel,
        grid=(n1, c // tc),
        in_specs=[blk, blk, blk, blk, mat, mat, tw, tw],
        out_specs=[blk, blk],
        out_shape=[out, out],
        compiler_params=_cparams(("parallel", "parallel")),
        name="dft_stage_c_conv",
    )(re3, im3, kre, kim, tab["fwd2"], tab["inv2"], tab["twc_c"], tab["twc_s"])


def _dft_a_inv_kernel(re_ref, im_ref, f_ref, x0_ref, g1_ref, bias_ref, o_ref, *, n1):
    width = o_ref.shape[2]
    for sg in range(o_ref.shape[1]):
        cols = slice(sg * width, (sg + 1) * width)
        y = (jnp.dot(f_ref[:, :n1], re_ref[:, cols], preferred_element_type=F32)
             + jnp.dot(f_ref[:, n1:], im_ref[:, cols], preferred_element_type=F32))
        o_ref[:, sg, :] = x0_ref[:, sg, :] * (y + bias_ref[...] * g1_ref[:, sg, :])


def _dft_stage_a_inv(re2, im2, tab, x0_3, g1_3, bias_row, *, seg=8):
    n1 = re2.shape[0]
    half, n2, width = x0_3.shape
    tn = seg * width
    blk = pl.BlockSpec((n1, tn), lambda j: (0, j))
    hblk = pl.BlockSpec((half, seg, width), lambda j: (0, j, 0))
    return pl.pallas_call(
        functools.partial(_dft_a_inv_kernel, n1=n1),
        grid=(n2 // seg,),
        in_specs=[blk, blk, pl.BlockSpec((half, 2 * n1), lambda j: (0, 0)), hblk, hblk,
                  pl.BlockSpec((1, width), lambda j: (0, 0))],
        out_specs=hblk,
        out_shape=jax.ShapeDtypeStruct((half, n2, width), F32),
        compiler_params=_cparams(("parallel",)),
        name="dft_stage_a_inv",
    )(re2, im2, tab["inv1"], x0_3, g1_3, bias_row)


def _hyena(p, col0, conv_w, conv_b, w1, b1, f1, w2, b2, f2, w3, bias):
    l = p.shape[0]
    c = HY_DIM
    n1 = 2 * l // DFT_N2
    tab = _dft_tables(n1)
    x0, g1 = _hyena_pre(p, col0, conv_w, conv_b)
    filt, abs_sum = _hyena_filter(l, w1, b1, f1, w2, b2, f2, w3)
    inv_norm = 1.0 / (abs_sum + 1e-6)
    fre, fim = _dft_stage_a(filt.reshape(n1, DFT_N2, c), tab, n1)
    kre, kim = _dft_stage_c_filter(fre.reshape(n1, DFT_N2, c), fim.reshape(n1, DFT_N2, c), tab,
                                   inv_norm)
    g1_3 = g1.reshape(n1 // 2, DFT_N2, c)
    are, aim = _dft_stage_a(g1_3, tab, n1)
    bre, bim = _dft_stage_c_conv(are.reshape(n1, DFT_N2, c), aim.reshape(n1, DFT_N2, c),
                                 kre, kim, tab)
    y = _dft_stage_a_inv(bre.reshape(n1, DFT_N2 * c), bim.reshape(n1, DFT_N2 * c), tab,
                         x0.reshape(n1 // 2, DFT_N2, c), g1_3, bias.reshape(1, c))
    return y.reshape(l, c)


def _mm2res_kernel(y1_ref, y2_ref, w1_ref, w2_ref, x_ref, gate_ref, o_ref):
    acc = jnp.dot(y1_ref[...].astype(BF16), w1_ref[...], preferred_element_type=F32)
    acc += jnp.dot(y2_ref[...].astype(BF16), w2_ref[...], preferred_element_type=F32)
    o_ref[...] = x_ref[...] + gate_ref[...] * acc


def _matmul2_residual(y1, y2, w, x, gate, *, tm=1024, tn=1024):
    m, k1 = y1.shape
    k2 = y2.shape[1]
    n = w.shape[1]
    tm = _tile(m, tm, 8)
    tn = _tile(n, tn)
    kb = k1 // k2
    assert k1 == kb * k2
    return pl.pallas_call(
        _mm2res_kernel,
        grid=(m // tm, n // tn),
        in_specs=[pl.BlockSpec((tm, k1), lambda i, j: (i, 0)),
                  pl.BlockSpec((tm, k2), lambda i, j: (i, 0)),
                  pl.BlockSpec((k1, tn), lambda i, j: (0, j)),
                  pl.BlockSpec((k2, tn), lambda i, j: (kb, j)),
                  pl.BlockSpec((tm, tn), lambda i, j: (i, j)),
                  pl.BlockSpec((1, tn), lambda i, j: (0, j))],
        out_specs=pl.BlockSpec((tm, tn), lambda i, j: (i, j)),
        out_shape=jax.ShapeDtypeStruct((m, n), F32),
        compiler_params=_cparams(("parallel", "parallel")),
        name="matmul2_residual",
    )(y1, y2, w, w, x, gate.reshape(1, n).astype(F32))


def _mixer_gmlp_hyena(x, g, sh, sc, gate, w_in, gm_norm_g, gm_ws, gm_b, hy_conv_w, hy_conv_b,
                      hy_w1, hy_b1, hy_f1, hy_w2, hy_b2, hy_f2, hy_w3, hy_bias, w_out):
    p = _norm_mod_matmul(x, g, sh, sc, w_in.astype(BF16))
    y_gm = _gmlp(p, gm_norm_g, gm_ws, gm_b)
    y_hy = _hyena(p, 2 * GM_DIM, hy_conv_w, hy_conv_b, hy_w1, hy_b1, hy_f1, hy_w2, hy_b2,
                  hy_f2, hy_w3, hy_bias)
    return _matmul2_residual(y_gm, y_hy, w_out.astype(BF16), x, gate)


def kernel(x, c, ctx, c_ctx, ln1_g0, w_mod0, b_mod0, w_in0, conv_w0, q_norm_g0, w_uq0, kv_norm_g0, w_ukv0, w_out0, ln2_g0, peer_wq0, peer_keys0, peer_u0, peer_v0, ln1_g1, w_mod1, b_mod1, w_in1, gm_norm_g1, gm_ws1, gm_b1, hy_conv_w1, hy_conv_b1, hy_w1, hy_b1, hy_f1, hy_w2, hy_b2, hy_f2, hy_w3, hy_bias1, w_out1, ln2_g1, peer_wq1, peer_keys1, peer_u1, peer_v1, final_g):
    b, l, d = x.shape
    assert b == 1 and d == D_MODEL
    xt = x.reshape(l, d)
    cos, sin = _rope_tables(l // GRID_W)
    rows = jnp.zeros((8, d), F32).at[0].set(c[0]).at[1].set(c_ctx)

    mod0 = _mod_vectors(rows, w_mod0, b_mod0)
    sh_m, sc_m, g_m, sh_f, sc_f, g_f = [mod0[0, j * d:(j + 1) * d] for j in range(N_MOD)]
    mc = mod0[1, :2 * d]
    xt = _mixer_conv_mla(xt, ctx[0], ln1_g0, sh_m, sc_m, mc, g_m, cos, sin, w_in0, conv_w0,
                         q_norm_g0, w_uq0, kv_norm_g0, w_ukv0, w_out0)
    xt = _peer(xt, ln2_g0, sh_f, sc_f, g_f, peer_wq0, peer_keys0, peer_u0, peer_v0, final_g,
               final_norm=False)

    mod1 = _mod_vectors(rows, w_mod1, b_mod1)
    sh_m, sc_m, g_m, sh_f, sc_f, g_f = [mod1[0, j * d:(j + 1) * d] for j in range(N_MOD)]
    xt = _mixer_gmlp_hyena(xt, ln1_g1, sh_m, sc_m, g_m, w_in1, gm_norm_g1, gm_ws1, gm_b1,
                           hy_conv_w1, hy_conv_b1, hy_w1, hy_b1, hy_f1, hy_w2, hy_b2, hy_f2,
                           hy_w3, hy_bias1, w_out1)
    xt = _peer(xt, ln2_g1, sh_f, sc_f, g_f, peer_wq1, peer_keys1, peer_u1, peer_v1, final_g,
               final_norm=True)
    return xt.reshape(b, l, d)
```
